```python
import math
import jax, jax.numpy as jnp
from jax import lax
import numpy as np

D_MODEL = 1024
BATCH = 4
SEQ = 8192
DEPTH = 1
DEC_BATCH = 128
DEC_SEQ = 8
PAST_LEN = 8192
PAGE_SIZE = 128

D_MIX = D_MODEL
D_CONV = D_MIX // 2
D_ATTN = D_MIX - D_CONV
N_HEADS = 4
HEAD_DIM = 64
V_DIM = 2 * HEAD_DIM
CONV_W = 3
Q_BLOCK = 128
EPS = 1e-6
SPLIT_WIDTHS = (D_CONV, D_CONV, D_CONV, D_CONV,
                N_HEADS * 2 * HEAD_DIM, N_HEADS * 2 * HEAD_DIM, N_HEADS * V_DIM, D_ATTN)
D_IN = sum(SPLIT_WIDTHS)
SPLITS = [int(s) for s in np.cumsum(SPLIT_WIDTHS)[:-1]]

kernel_name = "hymba_shortconv_diffattn_step"


def rmsnorm(x, g):
    xf = x.astype(jnp.float32)
    xf = xf * lax.rsqrt(jnp.mean(xf * xf, axis=-1, keepdims=True) + EPS)
    return (xf * g.astype(jnp.float32)).astype(x.dtype)


def lambda_init_fn(layer):
    return 0.8 - 0.6 * math.exp(-0.3 * layer)


def diff_lambda(lq1, lk1, lq2, lk2, lam_init):
    f = jnp.float32
    return (jnp.exp(jnp.sum(lq1.astype(f) * lk1.astype(f)))
            - jnp.exp(jnp.sum(lq2.astype(f) * lk2.astype(f))) + lam_init)


def diff_combine(s, lam):
    p = jax.nn.softmax(s, axis=-1)
    return p[:, :, 0] - lam * p[:, :, 1]


def short_conv(b, c, xc, buf, conv_w):
    u = c * xc
    full = jnp.concatenate([buf.astype(u.dtype), u], axis=1)
    t = u.shape[1]
    y = sum(conv_w[j] * full[:, j:j + t] for j in range(CONV_W))
    return b * y, full[:, -(CONV_W - 1):]


def prompt_attention(q, k, v, lam):
    n, s_len = q.shape[:2]
    nb = s_len // Q_BLOCK
    qb = jnp.moveaxis(q.reshape(n, nb, Q_BLOCK, N_HEADS, 2, HEAD_DIM), 1, 0)
    kpos = jnp.arange(s_len)

    def block(args):
        qi, i = args
        s = jnp.einsum('bqhcd,bkhcd->bhcqk', qi, k, preferred_element_type=jnp.float32)
        qpos = i * Q_BLOCK + jnp.arange(Q_BLOCK)
        s = jnp.where(kpos[None, :] <= qpos[:, None], s, -jnp.inf)
        a = diff_combine(s, lam).astype(v.dtype)
        return jnp.einsum('bhqk,bkhv->bqhv', a, v)

    o = lax.map(block, (qb, jnp.arange(nb)))
    return jnp.moveaxis(o, 0, 1).reshape(n, s_len, N_HEADS, V_DIM)


def sample_attention(q, k_new, v_new, k_past, v_past, lam):
    t = q.shape[1]
    p_len = k_past.shape[1]
    s_past = jnp.einsum('bqhcd,bkhcd->bhcqk', q, k_past, preferred_element_type=jnp.float32)
    s_new = jnp.einsum('bqhcd,bkhcd->bhcqk', q, k_new, preferred_element_type=jnp.float32)
    causal = jnp.tril(jnp.ones((t, t), dtype=bool))
    s_new = jnp.where(causal, s_new, -jnp.inf)
    a = diff_combine(jnp.concatenate([s_past, s_new], axis=-1), lam).astype(v_new.dtype)
    return (jnp.einsum('bhqk,bkhv->bqhv', a[..., :p_len], v_past)
            + jnp.einsum('bhqk,bkhv->bqhv', a[..., p_len:], v_new))


def mixer_layer(x, buf, attend, ln_g, w_in, conv_w, lam, lam_init, subln_g, w_out):
    n, t = x.shape[:2]
    h = rmsnorm(x, ln_g)
    b, c, xc, zc, q, k, v, za = jnp.split(h @ w_in, SPLITS, axis=-1)
    yc, new_buf = short_conv(b, c, xc, buf, conv_w)
    q = q.reshape(n, t, N_HEADS, 2, HEAD_DIM) * (HEAD_DIM ** -0.5)
    k = k.reshape(n, t, N_HEADS, 2, HEAD_DIM)
    v = v.reshape(n, t, N_HEADS, V_DIM)
    o = attend(q, k, v, lam)
    ya = (rmsnorm(o, subln_g) * (1.0 - lam_init)).reshape(n, t, D_ATTN)
    mix = jnp.concatenate([jax.nn.silu(zc) * yc, jax.nn.silu(za) * ya], axis=-1)
    return x + mix @ w_out, k.reshape(n, t, N_HEADS, 2 * HEAD_DIM), v, new_buf


def setup_inputs(seed: int = 0) -> dict:
    key = jax.random.key(seed)
    ks = jax.random.split(key, 20)
    n_pages = PAST_LEN // PAGE_SIZE
    n_used = DEC_BATCH * n_pages
    n_phys = n_used + n_used // 4
    f = jnp.float32
    x_prompt = jax.random.normal(ks[0], (BATCH, SEQ, D_MODEL), f)
    x_sample = jax.random.normal(ks[1], (DEC_BATCH, DEC_SEQ, D_MODEL), f)
    cache_k = jax.random.normal(ks[2], (DEPTH, n_phys, PAGE_SIZE, N_HEADS, 2 * HEAD_DIM), f)
    cache_v = jax.random.normal(ks[3], (DEPTH, n_phys, PAGE_SIZE, N_HEADS, V_DIM), f)
    state_conv = jax.random.normal(ks[4], (DEPTH, DEC_BATCH, CONV_W - 1, D_CONV), f)
    page_table = jax.random.permutation(ks[5], n_phys)[:n_used].reshape(DEC_BATCH, n_pages).astype(jnp.int32)
    ln_in_g = 1.0 + 0.01 * jax.random.normal(ks[6], (DEPTH, D_MODEL), f)
    w_in = jax.random.normal(ks[7], (DEPTH, D_MODEL, D_IN), f) * D_MODEL ** -0.5
    conv_w = jax.random.normal(ks[8], (DEPTH, CONV_W, D_CONV), f) * CONV_W ** -0.5
    lambda_q1 = 0.1 * jax.random.normal(ks[9], (DEPTH, HEAD_DIM), f)
    lambda_k1 = 0.1 * jax.random.normal(ks[10], (DEPTH, HEAD_DIM), f)
    lambda_q2 = 0.1 * jax.random.normal(ks[11], (DEPTH, HEAD_DIM), f)
    lambda_k2 = 0.1 * jax.random.normal(ks[12], (DEPTH, HEAD_DIM), f)
    subln_g = 1.0 + 0.01 * jax.random.normal(ks[13], (DEPTH, V_DIM), f)
    w_out = jax.random.normal(ks[14], (DEPTH, D_MIX, D_MODEL), f) * D_MIX ** -0.5
    ln_f_g = 1.0 + 0.01 * jax.random.normal(ks[15], (D_MODEL,), f)
    return {"x_prompt": x_prompt, "x_sample": x_sample, "cache_k": cache_k, "cache_v": cache_v,
            "state_conv": state_conv, "page_table": page_table, "ln_in_g": ln_in_g, "w_in": w_in,
            "conv_w": conv_w, "lambda_q1": lambda_q1, "lambda_k1": lambda_k1, "lambda_q2": lambda_q2,
            "lambda_k2": lambda_k2, "subln_g": subln_g, "w_out": w_out, "ln_f_g": ln_f_g}


def reference(x_prompt, x_sample, cache_k, cache_v, state_conv, page_table, ln_in_g, w_in,
              conv_w, lambda_q1, lambda_k1, lambda_q2, lambda_k2, subln_g, w_out, ln_f_g):
    n_pages = page_table.shape[1]
    past = n_pages * PAGE_SIZE
    nb_prompt = SEQ // PAGE_SIZE
    xp, xs = x_prompt, x_sample
    kp_l, vp_l, cp_l, ks_l, vs_l, cs_l = [], [], [], [], [], []
    for layer in range(DEPTH):
        lam_init = lambda_init_fn(layer)
        lam = diff_lambda(lambda_q1[layer], lambda_k1[layer], lambda_q2[layer], lambda_k2[layer], lam_init)
        params = (ln_in_g[layer], w_in[layer], conv_w[layer], lam, lam_init, subln_g[layer], w_out[layer])

        buf0 = jnp.zeros((xp.shape[0], CONV_W - 1, D_CONV), xp.dtype)
        xp, k_p, v_p, c_p = mixer_layer(xp, buf0, prompt_attention, *params)
        kp_l.append(k_p.reshape(xp.shape[0], nb_prompt, PAGE_SIZE, N_HEADS, 2 * HEAD_DIM))
        vp_l.append(v_p.reshape(xp.shape[0], nb_prompt, PAGE_SIZE, N_HEADS, V_DIM))
        cp_l.append(c_p)

        k_past = cache_k[layer][page_table].reshape(DEC_BATCH, past, N_HEADS, 2, HEAD_DIM)
        v_past = cache_v[layer][page_table].reshape(DEC_BATCH, past, N_HEADS, V_DIM)

        def attend_sample(q, k, v, lam_, k_past=k_past, v_past=v_past):
            return sample_attention(q, k, v, k_past, v_past, lam_)

        xs, k_s, v_s, c_s = mixer_layer(xs, state_conv[layer], attend_sample, *params)
        ks_l.append(k_s)
        vs_l.append(v_s)
        cs_l.append(c_s)

    y_prompt = rmsnorm(xp, ln_f_g)
    y_sample = rmsnorm(xs, ln_f_g)
    return (y_prompt, y_sample, jnp.stack(kp_l), jnp.stack(vp_l), jnp.stack(cp_l),
            jnp.stack(ks_l), jnp.stack(vs_l), jnp.stack(cs_l))
```

```python
import functools
import math

import jax
import jax.numpy as jnp
from jax import lax
from jax.experimental import pallas as pl
from jax.experimental.pallas import tpu as pltpu

D_MODEL = 1024
D_CONV = 512
D_ATTN = 512
N_HEADS = 4
HEAD_DIM = 64
V_DIM = 2 * HEAD_DIM
CONV_W = 3
PAGE_SIZE = 128
EPS = 1e-6
GROUP_W = 512
N_GROUPS = 8
SUBLANES = 8
LANES = 128
NEG_BIG = -1e30
VMEM_LIMIT = 56 * 1024 * 1024

F32 = jnp.float32
BF16 = jnp.bfloat16


def _lambda_init(layer):
    return 0.8 - 0.6 * math.exp(-0.3 * layer)


def _silu(z):
    return z / (1.0 + jnp.exp(-z))


def _rmsnorm(x, g):
    ms = jnp.mean(x * x, axis=-1, keepdims=True)
    return x * lax.rsqrt(ms + EPS) * g


def _proj(hb, w_ref, group):
    w = w_ref[:, group * GROUP_W:(group + 1) * GROUP_W]
    return jnp.dot(hb, w, preferred_element_type=F32)


def _diff_lambda(lam_ref, lam_init):
    lv = lam_ref[...]
    a = jnp.sum(lv[0:1] * lv[1:2], axis=-1, keepdims=True)
    b = jnp.sum(lv[2:3] * lv[3:4], axis=-1, keepdims=True)
    return jnp.exp(a) - jnp.exp(b) + lam_init


def _online_softmax_step(s, v_bf, m_ref, l_ref, acc_ref):
    m_old = m_ref[...]
    m_new = jnp.maximum(m_old, jnp.max(s, axis=-1, keepdims=True))
    alpha = jnp.exp(m_old - m_new)
    p = jnp.exp(s - pltpu.repeat(m_new, s.shape[1] // LANES, axis=1))
    l_ref[...] = alpha * l_ref[...] + jnp.sum(p, axis=-1, keepdims=True)
    pv = jnp.dot(p.astype(BF16), v_bf, preferred_element_type=F32)
    acc_ref[...] = pltpu.repeat(alpha, acc_ref.shape[1] // LANES, axis=1) * acc_ref[...] + pv
    m_ref[...] = m_new


def _subln_gate(o, sg, gate, lam_init):
    ms = jnp.mean(o * o, axis=-1, keepdims=True)
    return gate * (o * lax.rsqrt(ms + EPS) * sg * (1.0 - lam_init))


def _inproj_prompt_kernel(x_ref, g_ref, w_ref, cw_ref,
                          q_ref, k_ref, v_ref, kb_ref, vb_ref, mixc_ref, gate_ref, conv_ref,
                          ubuf, *, tiles_per_seq, tm):
    t = pl.program_id(0)
    hb = _rmsnorm(x_ref[...], g_ref[...]).astype(BF16)

    @pl.when(t % tiles_per_seq == 0)
    def _():
        ubuf[0:SUBLANES, :] = jnp.zeros((SUBLANES, D_CONV), F32)

    u = _proj(hb, w_ref, 1) * _proj(hb, w_ref, 2)
    ubuf[SUBLANES:SUBLANES + tm, :] = u
    cw = cw_ref[...]
    y = (cw[0:1] * ubuf[SUBLANES - 2:SUBLANES - 2 + tm, :]
         + cw[1:2] * ubuf[SUBLANES - 1:SUBLANES - 1 + tm, :]
         + cw[2:3] * u)
    yc = _proj(hb, w_ref, 0) * y
    mixc_ref[...] = (_silu(_proj(hb, w_ref, 3)) * yc).astype(BF16)
    ubuf[0:SUBLANES, :] = u[tm - SUBLANES:tm, :]

    @pl.when(t % tiles_per_seq == tiles_per_seq - 1)
    def _():
        conv_ref[0] = u[tm - (CONV_W - 1):tm, :]

    q_ref[...] = (_proj(hb, w_ref, 4) * (HEAD_DIM ** -0.5)).astype(BF16)
    k = _proj(hb, w_ref, 5)
    k_ref[...] = k
    kb_ref[...] = k.astype(BF16)
    v = _proj(hb, w_ref, 6)
    v_ref[...] = v
    vb_ref[...] = v.astype(BF16)
    gate_ref[...] = _silu(_proj(hb, w_ref, 7)).astype(BF16)


def _inproj_prompt(x2d, ln_g, w_bf, conv_w, n_seq, seq_len):
    n_tok = x2d.shape[0]
    tm = 512
    tiles_per_seq = seq_len // tm
    n_tiles = n_tok // tm
    tok_spec = lambda width: pl.BlockSpec((tm, width), lambda t: (t, 0))
    full = lambda shape: pl.BlockSpec(shape, lambda t: (0,) * len(shape))
    out_shape = (
        jax.ShapeDtypeStruct((n_tok, D_ATTN), BF16),
        jax.ShapeDtypeStruct((n_tok, D_ATTN), F32),
        jax.ShapeDtypeStruct((n_tok, D_ATTN), F32),
        jax.ShapeDtypeStruct((n_tok, D_ATTN), BF16),
        jax.ShapeDtypeStruct((n_tok, D_ATTN), BF16),
        jax.ShapeDtypeStruct((n_tok, D_CONV), BF16),
        jax.ShapeDtypeStruct((n_tok, D_ATTN), BF16),
        jax.ShapeDtypeStruct((n_seq, CONV_W - 1, D_CONV), F32),
    )
    return pl.pallas_call(
        functools.partial(_inproj_prompt_kernel, tiles_per_seq=tiles_per_seq, tm=tm),
        out_shape=out_shape,
        grid=(n_tiles,),
        in_specs=[tok_spec(D_MODEL), full((1, D_MODEL)), full((D_MODEL, N_GROUPS * GROUP_W)),
                  full((CONV_W, D_CONV))],
        out_specs=(tok_spec(D_ATTN),) * 7 + (
            pl.BlockSpec((1, CONV_W - 1, D_CONV), lambda t: (t // tiles_per_seq, 0, 0)),),
        scratch_shapes=[pltpu.VMEM((tm + SUBLANES, D_CONV), F32)],
        compiler_params=pltpu.CompilerParams(dimension_semantics=("arbitrary",),
                                             vmem_limit_bytes=VMEM_LIMIT),
        name="inproj_prompt",
    )(x2d, ln_g, w_bf, conv_w)


def _inproj_sample_kernel(x_ref, g_ref, w_ref, cw_ref, buf_ref,
                          q_ref, k_ref, v_ref, mixc_ref, gate_ref, conv_ref, *, n_seq, t_len):
    hb = _rmsnorm(x_ref[...], g_ref[...]).astype(BF16)
    n_tok = n_seq * t_len
    u3 = (_proj(hb, w_ref, 1) * _proj(hb, w_ref, 2)).reshape(n_seq, t_len, D_CONV)
    tpos = lax.broadcasted_iota(jnp.int32, (n_seq, t_len, D_CONV), 1)
    b0 = buf_ref[:, 0:1, :]
    b1 = buf_ref[:, 1:2, :]
    prev1 = jnp.where(tpos >= 1, pltpu.roll(u3, 1, 1), b1)
    prev2 = jnp.where(tpos >= 2, pltpu.roll(u3, 2, 1), jnp.where(tpos == 1, b1, b0))
    cw = cw_ref[...]
    y = cw[0:1][None] * prev2 + cw[1:2][None] * prev1 + cw[2:3][None] * u3
    yc = _proj(hb, w_ref, 0) * y.reshape(n_tok, D_CONV)
    mixc_ref[...] = _silu(_proj(hb, w_ref, 3)) * yc
    conv_ref[...] = u3[:, t_len - (CONV_W - 1):t_len, :]
    q_ref[...] = _proj(hb, w_ref, 4) * (HEAD_DIM ** -0.5)
    k_ref[...] = _proj(hb, w_ref, 5)
    v_ref[...] = _proj(hb, w_ref, 6)
    gate_ref[...] = _silu(_proj(hb, w_ref, 7))


def _inproj_sample(x2d, ln_g, w_bf, conv_w, buf, n_seq, t_len):
    assert t_len == SUBLANES
    n_tok = x2d.shape[0]
    tok = jax.ShapeDtypeStruct((n_tok, D_ATTN), F32)
    return pl.pallas_call(
        functools.partial(_inproj_sample_kernel, n_seq=n_seq, t_len=t_len),
        out_shape=(tok, tok, tok, tok, tok,
                   jax.ShapeDtypeStruct((n_seq, CONV_W - 1, D_CONV), F32)),
        compiler_params=pltpu.CompilerParams(vmem_limit_bytes=VMEM_LIMIT),
        name="inproj_sample",
    )(x2d, ln_g, w_bf, conv_w, buf)


def _attn_prompt_kernel(lam_ref, sg_ref, q_ref, k_ref, v_ref, gate_ref, o_ref,
                        qs_ref, m_ref, l_ref, acc_ref, *, tq, tk, lam_init):
    qi = pl.program_id(2)
    q_start = qi * tq
    q = q_ref[...]
    lane = lax.broadcasted_iota(jnp.int32, (tq, V_DIM), 1)
    zero = jnp.zeros_like(q)
    qs_ref[0:tq, :] = jnp.where(lane < HEAD_DIM, q, zero)
    qs_ref[tq:2 * tq, :] = jnp.where(lane >= HEAD_DIM, q, zero)
    m_ref[...] = jnp.full(m_ref.shape, NEG_BIG, F32)
    l_ref[...] = jnp.zeros(l_ref.shape, F32)
    acc_ref[...] = jnp.zeros(acc_ref.shape, F32)

    def scores(ki):
        k_off = pl.multiple_of(ki * tk, tk)
        kblk = k_ref[pl.ds(k_off, tk), :]
        vblk = v_ref[pl.ds(k_off, tk), :]
        s = lax.dot_general(qs_ref[...], kblk, (((1,), (1,)), ((), ())),
                            preferred_element_type=F32)
        return s, vblk

    def full_tile(ki, carry):
        s, vblk = scores(ki)
        _online_softmax_step(s, vblk, m_ref, l_ref, acc_ref)
        return carry

    n_full = q_start // tk
    lax.fori_loop(0, n_full, full_tile, 0)

    s, vblk = scores(n_full)
    row = lax.broadcasted_iota(jnp.int32, (2 * tq, tk), 0)
    col = lax.broadcasted_iota(jnp.int32, (2 * tq, tk), 1)
    q_pos = q_start + jnp.where(row >= tq, row - tq, row)
    s = jnp.where(n_full * tk + col <= q_pos, s, NEG_BIG)
    _online_softmax_step(s, vblk, m_ref, l_ref, acc_ref)

    acc = acc_ref[...]
    l = l_ref[...]
    o = acc[0:tq] / l[0:tq] - _diff_lambda(lam_ref, lam_init) * (acc[tq:2 * tq] / l[tq:2 * tq])
    o_ref[...] = _subln_gate(o, sg_ref[...], gate_ref[...].astype(F32), lam_init).astype(BF16)


def _attn_prompt(lam_rows, subln_g, q_bf, k_bf, v_bf, gate_bf, n_seq, seq_len, lam_init):
    tq, tk = 256, 512
    assert tk % tq == 0 and seq_len % tk == 0
    nq = seq_len // tq
    n_tok = q_bf.shape[0]
    q_spec = pl.BlockSpec((tq, V_DIM), lambda b, h, i: (b * nq + i, h))
    kv_spec = pl.BlockSpec((seq_len, V_DIM), lambda b, h, i: (b, h))
    full = lambda shape: pl.BlockSpec(shape, lambda b, h, i: (0,) * len(shape))
    return pl.pallas_call(
        functools.partial(_attn_prompt_kernel, tq=tq, tk=tk, lam_init=lam_init),
        out_shape=jax.ShapeDtypeStruct((n_tok, D_ATTN), BF16),
        grid=(n_seq, N_HEADS, nq),
        in_specs=[full((4, HEAD_DIM)), full((1, V_DIM)), q_spec, kv_spec, kv_spec, q_spec],
        out_specs=q_spec,
        scratch_shapes=[pltpu.VMEM((2 * tq, V_DIM), BF16),
                        pltpu.VMEM((2 * tq, LANES), F32),
                        pltpu.VMEM((2 * tq, LANES), F32),
                        pltpu.VMEM((2 * tq, V_DIM), F32)],
        compiler_params=pltpu.CompilerParams(
            dimension_semantics=("arbitrary", "arbitrary", "arbitrary"),
            vmem_limit_bytes=VMEM_LIMIT),
        name="attn_prompt",
    )(lam_rows, subln_g, q_bf, k_bf, v_bf, gate_bf)


def _attn_sample_kernel(pt_ref, lam_ref, sg_ref, q_ref, kn_ref, vn_ref, gate_ref, *rest,
                        pages_per_step, t_len, lam_init):
    del pt_ref
    g_pages = pages_per_step
    k_refs = rest[:g_pages]
    v_refs = rest[g_pages:2 * g_pages]
    o_ref = rest[2 * g_pages]
    qbd_ref, m_ref, l_ref, acc_ref, kbuf, vbuf = rest[2 * g_pages + 1:]
    j = pl.program_id(1)
    n_rows = N_HEADS * 2 * t_len

    @pl.when(j == 0)
    def _():
        qt = jnp.concatenate([q_ref[...]] * (N_HEADS * 2), axis=0)
        row = lax.broadcasted_iota(jnp.int32, (n_rows, D_ATTN), 0)
        col = lax.broadcasted_iota(jnp.int32, (n_rows, D_ATTN), 1)
        keep = (col // HEAD_DIM) == (row // t_len)
        qbd_ref[...] = jnp.where(keep, qt, 0.0).astype(BF16)
        m_ref[...] = jnp.full(m_ref.shape, NEG_BIG, F32)
        l_ref[...] = jnp.zeros(l_ref.shape, F32)
        acc_ref[...] = jnp.zeros(acc_ref.shape, F32)

    for g in range(g_pages):
        kbuf[g * PAGE_SIZE:(g + 1) * PAGE_SIZE, :] = k_refs[g][0].astype(BF16)
        vbuf[g * PAGE_SIZE:(g + 1) * PAGE_SIZE, :] = v_refs[g][0].astype(BF16)
    s = lax.dot_general(qbd_ref[...], kbuf[...], (((1,), (1,)), ((), ())),
                        preferred_element_type=F32)
    _online_softmax_step(s, vbuf[...], m_ref, l_ref, acc_ref)

    @pl.when(j == pl.num_programs(1) - 1)
    def _():
        pad = jnp.zeros((LANES - t_len, D_ATTN), F32)
        kn = jnp.concatenate([kn_ref[...], pad], axis=0).astype(BF16)
        vn = jnp.concatenate([vn_ref[...], pad], axis=0).astype(BF16)
        sn = lax.dot_general(qbd_ref[...], kn, (((1,), (1,)), ((), ())),
                             preferred_element_type=F32)
        row = lax.broadcasted_iota(jnp.int32, (n_rows, LANES), 0)
        col = lax.broadcasted_iota(jnp.int32, (n_rows, LANES), 1)
        sn = jnp.where(col <= row % t_len, sn, NEG_BIG)
        _online_softmax_step(sn, vn, m_ref, l_ref, acc_ref)

        lam = _diff_lambda(lam_ref, lam_init)
        acc = acc_ref[...]
        l = l_ref[...]
        sg = sg_ref[...]
        gate = gate_ref[...]
        for h in range(N_HEADS):
            r1 = (2 * h) * t_len
            r2 = (2 * h + 1) * t_len
            lanes = slice(h * V_DIM, (h + 1) * V_DIM)
            o = (acc[r1:r1 + t_len, lanes] / l[r1:r1 + t_len]
                 - lam * (acc[r2:r2 + t_len, lanes] / l[r2:r2 + t_len]))
            o_ref[:, lanes] = _subln_gate(o, sg, gate[:, lanes], lam_init)


def _attn_sample(page_table, lam_rows, subln_g, q, k_new, v_new, gate, cache_k, cache_v,
                 n_seq, t_len, lam_init):
    assert t_len == SUBLANES and V_DIM == LANES
    n_pages = page_table.shape[1]
    g_pages = 16
    assert n_pages % g_pages == 0
    n_steps = n_pages // g_pages
    n_rows = N_HEADS * 2 * t_len
    n_tok = n_seq * t_len
    pt_flat = page_table.reshape(-1)
    ck = cache_k.reshape(cache_k.shape[0], PAGE_SIZE, D_ATTN)
    cv = cache_v.reshape(cache_v.shape[0], PAGE_SIZE, D_ATTN)

    def page_spec(g):
        return pl.BlockSpec(
            (1, PAGE_SIZE, D_ATTN),
            lambda b, j, pt: (pt[b * n_pages + j * g_pages + g], 0, 0))

    tok_spec = pl.BlockSpec((t_len, D_ATTN), lambda b, j, pt: (b, 0))
    full = lambda shape: pl.BlockSpec(shape, lambda b, j, pt: (0,) * len(shape))
    grid_spec = pltpu.PrefetchScalarGridSpec(
        num_scalar_prefetch=1,
        grid=(n_seq, n_steps),
        in_specs=[full((4, HEAD_DIM)), full((1, V_DIM)), tok_spec, tok_spec, tok_spec, tok_spec]
                 + [page_spec(g) for g in range(g_pages)]
                 + [page_spec(g) for g in range(g_pages)],
        out_specs=tok_spec,
        scratch_shapes=[pltpu.VMEM((n_rows, D_ATTN), BF16),
                        pltpu.VMEM((n_rows, LANES), F32),
                        pltpu.VMEM((n_rows, LANES), F32),
                        pltpu.VMEM((n_rows, D_ATTN), F32),
                        pltpu.VMEM((g_pages * PAGE_SIZE, D_ATTN), BF16),
                        pltpu.VMEM((g_pages * PAGE_SIZE, D_ATTN), BF16)],
    )
    return pl.pallas_call(
        functools.partial(_attn_sample_kernel, pages_per_step=g_pages, t_len=t_len,
                          lam_init=lam_init),
        out_shape=jax.ShapeDtypeStruct((n_tok, D_ATTN), F32),
        grid_spec=grid_spec,
        compiler_params=pltpu.CompilerParams(
            dimension_semantics=("arbitrary", "arbitrary"), vmem_limit_bytes=VMEM_LIMIT),
        name="attn_sample",
    )(pt_flat, lam_rows, subln_g, q, k_new, v_new, gate, *([ck] * g_pages), *([cv] * g_pages))


def _outproj_kernel(x_ref, mc_ref, ma_ref, w_ref, g_ref, y_ref, *, final_norm):
    acc = jnp.dot(mc_ref[...].astype(BF16), w_ref[0:D_CONV, :], preferred_element_type=F32)
    acc = acc + jnp.dot(ma_ref[...].astype(BF16), w_ref[D_CONV:D_CONV + D_ATTN, :],
                        preferred_element_type=F32)
    y = x_ref[...] + acc
    y_ref[...] = _rmsnorm(y, g_ref[...]) if final_norm else y


def _outproj(x2d, mixc, mixa, w_bf, ln_f_g, final_norm):
    n_tok = x2d.shape[0]
    tm = 512
    tok_spec = lambda width: pl.BlockSpec((tm, width), lambda t: (t, 0))
    full = lambda shape: pl.BlockSpec(shape, lambda t: (0,) * len(shape))
    return pl.pallas_call(
        functools.partial(_outproj_kernel, final_norm=final_norm),
        out_shape=jax.ShapeDtypeStruct((n_tok, D_MODEL), F32),
        grid=(n_tok // tm,),
        in_specs=[tok_spec(D_MODEL), tok_spec(D_CONV), tok_spec(D_ATTN),
                  full((D_CONV + D_ATTN, D_MODEL)), full((1, D_MODEL))],
        out_specs=tok_spec(D_MODEL),
        compiler_params=pltpu.CompilerParams(dimension_semantics=("arbitrary",),
                                             vmem_limit_bytes=VMEM_LIMIT),
        name="outproj",
    )(x2d, mixc, mixa, w_bf, ln_f_g)


def kernel(x_prompt, x_sample, cache_k, cache_v, state_conv, page_table, ln_in_g, w_in, conv_w,
           lambda_q1, lambda_k1, lambda_q2, lambda_k2, subln_g, w_out, ln_f_g):
    depth = w_in.shape[0]
    n_p, s_p, _ = x_prompt.shape
    n_s, t_s, _ = x_sample.shape
    xp = x_prompt.reshape(n_p * s_p, D_MODEL)
    xs = x_sample.reshape(n_s * t_s, D_MODEL)
    ln_f = ln_f_g.reshape(1, D_MODEL)
    kp_l, vp_l, cp_l, ks_l, vs_l, cs_l = [], [], [], [], [], []
    for layer in range(depth):
        lam_init = _lambda_init(layer)
        last = layer == depth - 1
        ln_g = ln_in_g[layer].reshape(1, D_MODEL)
        w_in_bf = w_in[layer].astype(BF16)
        w_out_bf = w_out[layer].astype(BF16)
        sg = subln_g[layer].reshape(1, V_DIM)
        lam_rows = jnp.stack([lambda_q1[layer], lambda_k1[layer],
                              lambda_q2[layer], lambda_k2[layer]])

        q_p, k_p, v_p, kb_p, vb_p, mixc_p, gate_p, conv_p = _inproj_prompt(
            xp, ln_g, w_in_bf, conv_w[layer], n_p, s_p)
        mixa_p = _attn_prompt(lam_rows, sg, q_p, kb_p, vb_p, gate_p, n_p, s_p, lam_init)
        xp = _outproj(xp, mixc_p, mixa_p, w_out_bf, ln_f, last)
        kp_l.append(k_p.reshape(n_p, s_p // PAGE_SIZE, PAGE_SIZE, N_HEADS, 2 * HEAD_DIM))
        vp_l.append(v_p.reshape(n_p, s_p // PAGE_SIZE, PAGE_SIZE, N_HEADS, V_DIM))
        cp_l.append(conv_p)

        q_s, k_s, v_s, mixc_s, gate_s, conv_s = _inproj_sample(
            xs, ln_g, w_in_bf, conv_w[layer], state_conv[layer], n_s, t_s)
        mixa_s = _attn_sample(page_table, lam_rows, sg, q_s, k_s, v_s, gate_s,
                              cache_k[layer], cache_v[layer], n_s, t_s, lam_init)
        xs = _outproj(xs, mixc_s, mixa_s, w_out_bf, ln_f, last)
        ks_l.append(k_s.reshape(n_s, t_s, N_HEADS, 2 * HEAD_DIM))
        vs_l.append(v_s.reshape(n_s, t_s, N_HEADS, V_DIM))
        cs_l.append(conv_s)

    y_prompt = xp.reshape(n_p, s_p, D_MODEL)
    y_sample = xs.reshape(n_s, t_s, D_MODEL)
    return (y_prompt, y_sample, jnp.stack(kp_l), jnp.stack(vp_l), jnp.stack(cp_l),
            jnp.stack(ks_l), jnp.stack(vs_l), jnp.stack(cs_l))
```

```python
import functools
import math

import jax
import jax.numpy as jnp
from jax import lax
from jax.experimental import pallas as pl
from jax.experimental.pallas import tpu as pltpu

D_MODEL = 1024
D_CONV = 512
D_ATTN = 512
N_HEADS = 4
HEAD_DIM = 64
V_DIM = 2 * HEAD_DIM
CONV_W = 3
PAGE_SIZE = 128
EPS = 1e-6
GROUP_W = 512
N_GROUPS = 8
SUBLANES = 8
LANES = 128
NEG_BIG = -1e30
VMEM_LIMIT = 56 * 1024 * 1024

F32 = jnp.float32
BF16 = jnp.bfloat16


def _lambda_init(layer):
    return 0.8 - 0.6 * math.exp(-0.3 * layer)


def _silu(z):
    return z / (1.0 + jnp.exp(-z))


def _rmsnorm(x, g):
    ms = jnp.mean(x * x, axis=-1, keepdims=True)
    return x * lax.rsqrt(ms + EPS) * g


def _proj(hb, w_ref, group):
    w = w_ref[:, group * GROUP_W:(group + 1) * GROUP_W]
    return jnp.dot(hb, w, preferred_element_type=F32)


def _diff_lambda(lam_ref, lam_init):
    lv = lam_ref[...]
    a = jnp.sum(lv[0:1] * lv[1:2], axis=-1, keepdims=True)
    b = jnp.sum(lv[2:3] * lv[3:4], axis=-1, keepdims=True)
    return jnp.exp(a) - jnp.exp(b) + lam_init


def _online_softmax(s, v_bf, m, l, acc):
    m_new = jnp.maximum(m, jnp.max(s, axis=-1, keepdims=True))
    alpha = jnp.exp(m - m_new)
    p = jnp.exp(s - pltpu.repeat(m_new, s.shape[1] // LANES, axis=1))
    l = alpha * l + jnp.sum(p, axis=-1, keepdims=True)
    pv = jnp.dot(p.astype(BF16), v_bf, preferred_element_type=F32)
    acc = pltpu.repeat(alpha, acc.shape[1] // LANES, axis=1) * acc + pv
    return m_new, l, acc


def _online_softmax_step(s, v_bf, m_ref, l_ref, acc_ref):
    m_ref[...], l_ref[...], acc_ref[...] = _online_softmax(
        s, v_bf, m_ref[...], l_ref[...], acc_ref[...])


def _store_head_rows(ref, x, n_tok):
    for h in range(N_HEADS):
        ref[pl.ds(h, n_tok, stride=N_HEADS), :] = x[:, h * V_DIM:(h + 1) * V_DIM]


def _load_head_rows(ref, n_tok):
    return jnp.concatenate(
        [ref[pl.ds(h, n_tok, stride=N_HEADS), :] for h in range(N_HEADS)], axis=1)


def _subln_gate(o, sg, gate, lam_init):
    ms = jnp.mean(o * o, axis=-1, keepdims=True)
    return gate * (o * lax.rsqrt(ms + EPS) * sg * (1.0 - lam_init))


def _inproj_prompt_kernel(x_ref, g_ref, w_ref, cw_ref,
                          q_ref, k_ref, v_ref, kb_ref, vb_ref, mixc_ref, gate_ref, conv_ref,
                          ubuf, *, tiles_per_seq, tm):
    t = pl.program_id(0)
    hb = _rmsnorm(x_ref[...], g_ref[...]).astype(BF16)

    @pl.when(t % tiles_per_seq == 0)
    def _():
        ubuf[0:SUBLANES, :] = jnp.zeros((SUBLANES, D_CONV), F32)

    u = _proj(hb, w_ref, 1) * _proj(hb, w_ref, 2)
    ubuf[SUBLANES:SUBLANES + tm, :] = u
    cw = cw_ref[...]
    y = (cw[0:1] * ubuf[SUBLANES - 2:SUBLANES - 2 + tm, :]
         + cw[1:2] * ubuf[SUBLANES - 1:SUBLANES - 1 + tm, :]
         + cw[2:3] * u)
    yc = _proj(hb, w_ref, 0) * y
    mixc_ref[...] = (_silu(_proj(hb, w_ref, 3)) * yc).astype(BF16)
    ubuf[0:SUBLANES, :] = u[tm - SUBLANES:tm, :]

    @pl.when(t % tiles_per_seq == tiles_per_seq - 1)
    def _():
        conv_ref[0] = u[tm - (CONV_W - 1):tm, :]

    q_ref[...] = (_proj(hb, w_ref, 4) * (HEAD_DIM ** -0.5)).astype(BF16)
    k = _proj(hb, w_ref, 5)
    _store_head_rows(k_ref, k, tm)
    kb_ref[...] = k.astype(BF16)
    v = _proj(hb, w_ref, 6)
    _store_head_rows(v_ref, v, tm)
    vb_ref[...] = v.astype(BF16)
    gate_ref[...] = _silu(_proj(hb, w_ref, 7)).astype(BF16)


def _inproj_prompt(x2d, ln_g, w_bf, conv_w, n_seq, seq_len):
    n_tok = x2d.shape[0]
    tm = 512
    tiles_per_seq = seq_len // tm
    n_tiles = n_tok // tm
    tok_spec = lambda width: pl.BlockSpec((tm, width), lambda t: (t, 0))
    head_rows = pl.BlockSpec((tm * N_HEADS, V_DIM), lambda t: (t, 0))
    full = lambda shape: pl.BlockSpec(shape, lambda t: (0,) * len(shape))
    out_shape = (
        jax.ShapeDtypeStruct((n_tok, D_ATTN), BF16),
        jax.ShapeDtypeStruct((n_tok * N_HEADS, V_DIM), F32),
        jax.ShapeDtypeStruct((n_tok * N_HEADS, V_DIM), F32),
        jax.ShapeDtypeStruct((n_tok, D_ATTN), BF16),
        jax.ShapeDtypeStruct((n_tok, D_ATTN), BF16),
        jax.ShapeDtypeStruct((n_tok, D_CONV), BF16),
        jax.ShapeDtypeStruct((n_tok, D_ATTN), BF16),
        jax.ShapeDtypeStruct((n_seq, CONV_W - 1, D_CONV), F32),
    )
    return pl.pallas_call(
        functools.partial(_inproj_prompt_kernel, tiles_per_seq=tiles_per_seq, tm=tm),
        out_shape=out_shape,
        grid=(n_tiles,),
        in_specs=[tok_spec(D_MODEL), full((1, D_MODEL)), full((D_MODEL, N_GROUPS * GROUP_W)),
                  full((CONV_W, D_CONV))],
        out_specs=(tok_spec(D_ATTN), head_rows, head_rows) + (tok_spec(D_ATTN),) * 4 + (
            pl.BlockSpec((1, CONV_W - 1, D_CONV), lambda t: (t // tiles_per_seq, 0, 0)),),
        scratch_shapes=[pltpu.VMEM((tm + SUBLANES, D_CONV), F32)],
        compiler_params=pltpu.CompilerParams(dimension_semantics=("arbitrary",),
                                             vmem_limit_bytes=VMEM_LIMIT),
        name="inproj_prompt",
    )(x2d, ln_g, w_bf, conv_w)


def _inproj_sample_kernel(x_ref, g_ref, w_ref, cw_ref, buf_ref,
                          q_ref, k_ref, v_ref, mixc_ref, gate_ref, conv_ref, *, n_seq, t_len):
    hb = _rmsnorm(x_ref[...], g_ref[...]).astype(BF16)
    n_tok = n_seq * t_len
    u3 = (_proj(hb, w_ref, 1) * _proj(hb, w_ref, 2)).reshape(n_seq, t_len, D_CONV)
    tpos = lax.broadcasted_iota(jnp.int32, (n_seq, t_len, D_CONV), 1)
    b0 = buf_ref[:, 0:1, :]
    b1 = buf_ref[:, 1:2, :]
    prev1 = jnp.where(tpos >= 1, pltpu.roll(u3, 1, 1), b1)
    prev2 = jnp.where(tpos >= 2, pltpu.roll(u3, 2, 1), jnp.where(tpos == 1, b1, b0))
    cw = cw_ref[...]
    y = cw[0:1][None] * prev2 + cw[1:2][None] * prev1 + cw[2:3][None] * u3
    yc = _proj(hb, w_ref, 0) * y.reshape(n_tok, D_CONV)
    mixc_ref[...] = _silu(_proj(hb, w_ref, 3)) * yc
    conv_ref[...] = u3[:, t_len - (CONV_W - 1):t_len, :]
    q_ref[...] = _proj(hb, w_ref, 4) * (HEAD_DIM ** -0.5)
    _store_head_rows(k_ref, _proj(hb, w_ref, 5), n_tok)
    _store_head_rows(v_ref, _proj(hb, w_ref, 6), n_tok)
    gate_ref[...] = _silu(_proj(hb, w_ref, 7))


def _inproj_sample(x2d, ln_g, w_bf, conv_w, buf, n_seq, t_len):
    assert t_len == SUBLANES
    n_tok = x2d.shape[0]
    tok = jax.ShapeDtypeStruct((n_tok, D_ATTN), F32)
    head_rows = jax.ShapeDtypeStruct((n_tok * N_HEADS, V_DIM), F32)
    return pl.pallas_call(
        functools.partial(_inproj_sample_kernel, n_seq=n_seq, t_len=t_len),
        out_shape=(tok, head_rows, head_rows, tok, tok,
                   jax.ShapeDtypeStruct((n_seq, CONV_W - 1, D_CONV), F32)),
        compiler_params=pltpu.CompilerParams(vmem_limit_bytes=VMEM_LIMIT),
        name="inproj_sample",
    )(x2d, ln_g, w_bf, conv_w, buf)


def _attn_prompt_kernel(lam_ref, sg_ref, q_ref, k_ref, v_ref, gate_ref, o_ref,
                        qs_ref, m_ref, l_ref, acc_ref, *, tq, tk, unroll, lam_init):
    qi = pl.program_id(2)
    q_start = qi * tq
    q = q_ref[...]
    lane = lax.broadcasted_iota(jnp.int32, (tq, V_DIM), 1)
    zero = jnp.zeros_like(q)
    qs_ref[0:tq, :] = jnp.where(lane < HEAD_DIM, q, zero)
    qs_ref[tq:2 * tq, :] = jnp.where(lane >= HEAD_DIM, q, zero)
    m_ref[...] = jnp.full(m_ref.shape, NEG_BIG, F32)
    l_ref[...] = jnp.zeros(l_ref.shape, F32)
    acc_ref[...] = jnp.zeros(acc_ref.shape, F32)

    def scores(ki):
        k_off = pl.multiple_of(ki * tk, tk)
        kblk = k_ref[pl.ds(k_off, tk), :]
        vblk = v_ref[pl.ds(k_off, tk), :]
        s = lax.dot_general(qs_ref[...], kblk, (((1,), (1,)), ((), ())),
                            preferred_element_type=F32)
        return s, vblk

    def full_tiles(first, count):
        state = (m_ref[...], l_ref[...], acc_ref[...])
        for u in range(count):
            s, vblk = scores(first + u)
            state = _online_softmax(s, vblk, *state)
        m_ref[...], l_ref[...], acc_ref[...] = state

    def group(gi, carry):
        full_tiles(gi * unroll, unroll)
        return carry

    n_full = q_start // tk
    n_groups = n_full // unroll
    lax.fori_loop(0, n_groups, group, 0)
    done = n_groups * unroll
    part = unroll // 2
    while part >= 1:
        take = (n_full - done) & part

        @pl.when(take != 0)
        def _(done=done, part=part):
            full_tiles(done, part)

        done = done + take
        part //= 2

    s, vblk = scores(n_full)
    row = lax.broadcasted_iota(jnp.int32, (2 * tq, tk), 0)
    col = lax.broadcasted_iota(jnp.int32, (2 * tq, tk), 1)
    q_pos = q_start + jnp.where(row >= tq, row - tq, row)
    s = jnp.where(n_full * tk + col <= q_pos, s, NEG_BIG)
    _online_softmax_step(s, vblk, m_ref, l_ref, acc_ref)

    acc = acc_ref[...]
    l = l_ref[...]
    o = acc[0:tq] / l[0:tq] - _diff_lambda(lam_ref, lam_init) * (acc[tq:2 * tq] / l[tq:2 * tq])
    o_ref[...] = _subln_gate(o, sg_ref[...], gate_ref[...].astype(F32), lam_init).astype(BF16)


def _attn_prompt(lam_rows, subln_g, q_bf, k_bf, v_bf, gate_bf, n_seq, seq_len, lam_init):
    tq, tk, unroll = 256, 512, 4
    assert tk % tq == 0 and seq_len % tk == 0 and unroll & (unroll - 1) == 0
    nq = seq_len // tq
    n_tok = q_bf.shape[0]
    q_spec = pl.BlockSpec((tq, V_DIM), lambda b, h, i: (b * nq + i, h))
    kv_spec = pl.BlockSpec((seq_len, V_DIM), lambda b, h, i: (b, h))
    full = lambda shape: pl.BlockSpec(shape, lambda b, h, i: (0,) * len(shape))
    return pl.pallas_call(
        functools.partial(_attn_prompt_kernel, tq=tq, tk=tk, unroll=unroll, lam_init=lam_init),
        out_shape=jax.ShapeDtypeStruct((n_tok, D_ATTN), BF16),
        grid=(n_seq, N_HEADS, nq),
        in_specs=[full((4, HEAD_DIM)), full((1, V_DIM)), q_spec, kv_spec, kv_spec, q_spec],
        out_specs=q_spec,
        scratch_shapes=[pltpu.VMEM((2 * tq, V_DIM), BF16),
                        pltpu.VMEM((2 * tq, LANES), F32),
                        pltpu.VMEM((2 * tq, LANES), F32),
                        pltpu.VMEM((2 * tq, V_DIM), F32)],
        compiler_params=pltpu.CompilerParams(
            dimension_semantics=("arbitrary", "arbitrary", "arbitrary"),
            vmem_limit_bytes=VMEM_LIMIT),
        name="attn_prompt",
    )(lam_rows, subln_g, q_bf, k_bf, v_bf, gate_bf)


def _attn_sample_kernel(pt_ref, lam_ref, sg_ref, q_ref, kn_ref, vn_ref, gate_ref, *rest,
                        pages_per_step, t_len, lam_init):
    del pt_ref
    g_pages = pages_per_step
    k_refs = rest[:g_pages]
    v_refs = rest[g_pages:2 * g_pages]
    o_ref = rest[2 * g_pages]
    qbd_ref, m_ref, l_ref, acc_ref, kbuf, vbuf = rest[2 * g_pages + 1:]
    j = pl.program_id(1)
    n_rows = N_HEADS * 2 * t_len

    @pl.when(j == 0)
    def _():
        qt = jnp.concatenate([q_ref[...]] * (N_HEADS * 2), axis=0)
        row = lax.broadcasted_iota(jnp.int32, (n_rows, D_ATTN), 0)
        col = lax.broadcasted_iota(jnp.int32, (n_rows, D_ATTN), 1)
        keep = (col // HEAD_DIM) == (row // t_len)
        qbd_ref[...] = jnp.where(keep, qt, 0.0).astype(BF16)
        m_ref[...] = jnp.full(m_ref.shape, NEG_BIG, F32)
        l_ref[...] = jnp.zeros(l_ref.shape, F32)
        acc_ref[...] = jnp.zeros(acc_ref.shape, F32)

    for g in range(g_pages):
        rows = slice(g * PAGE_SIZE, (g + 1) * PAGE_SIZE)
        kbuf[rows, :] = _load_head_rows(k_refs[g], PAGE_SIZE).astype(BF16)
        vbuf[rows, :] = _load_head_rows(v_refs[g], PAGE_SIZE).astype(BF16)
    s = lax.dot_general(qbd_ref[...], kbuf[...], (((1,), (1,)), ((), ())),
                        preferred_element_type=F32)
    _online_softmax_step(s, vbuf[...], m_ref, l_ref, acc_ref)

    @pl.when(j == pl.num_programs(1) - 1)
    def _():
        pad = jnp.zeros((LANES - t_len, D_ATTN), F32)
        kn = jnp.concatenate([_load_head_rows(kn_ref, t_len), pad], axis=0).astype(BF16)
        vn = jnp.concatenate([_load_head_rows(vn_ref, t_len), pad], axis=0).astype(BF16)
        sn = lax.dot_general(qbd_ref[...], kn, (((1,), (1,)), ((), ())),
                             preferred_element_type=F32)
        row = lax.broadcasted_iota(jnp.int32, (n_rows, LANES), 0)
        col = lax.broadcasted_iota(jnp.int32, (n_rows, LANES), 1)
        sn = jnp.where(col <= row % t_len, sn, NEG_BIG)
        _online_softmax_step(sn, vn, m_ref, l_ref, acc_ref)

        lam = _diff_lambda(lam_ref, lam_init)
        acc = acc_ref[...]
        l = l_ref[...]
        sg = sg_ref[...]
        gate = gate_ref[...]
        for h in range(N_HEADS):
            r1 = (2 * h) * t_len
            r2 = (2 * h + 1) * t_len
            lanes = slice(h * V_DIM, (h + 1) * V_DIM)
            o = (acc[r1:r1 + t_len, lanes] / l[r1:r1 + t_len]
                 - lam * (acc[r2:r2 + t_len, lanes] / l[r2:r2 + t_len]))
            o_ref[:, lanes] = _subln_gate(o, sg, gate[:, lanes], lam_init)


def _attn_sample(page_table, lam_rows, subln_g, q, k_new, v_new, gate, cache_k, cache_v, layer,
                 n_seq, t_len, lam_init):
    assert t_len == SUBLANES and V_DIM == LANES
    n_pages = page_table.shape[1]
    g_pages = 16
    assert n_pages % g_pages == 0
    n_steps = n_pages // g_pages
    n_rows = N_HEADS * 2 * t_len
    n_tok = n_seq * t_len
    pt_flat = page_table.reshape(-1)
    n_phys = cache_k.shape[1]
    page_rows = PAGE_SIZE * N_HEADS
    ck = cache_k.reshape(-1, V_DIM)
    cv = cache_v.reshape(-1, V_DIM)

    def page_spec(g):
        return pl.BlockSpec(
            (page_rows, V_DIM),
            lambda b, j, pt: (layer * n_phys + pt[b * n_pages + j * g_pages + g], 0))

    tok_spec = pl.BlockSpec((t_len, D_ATTN), lambda b, j, pt: (b, 0))
    new_spec = pl.BlockSpec((t_len * N_HEADS, V_DIM), lambda b, j, pt: (b, 0))
    full = lambda shape: pl.BlockSpec(shape, lambda b, j, pt: (0,) * len(shape))
    grid_spec = pltpu.PrefetchScalarGridSpec(
        num_scalar_prefetch=1,
        grid=(n_seq, n_steps),
        in_specs=[full((4, HEAD_DIM)), full((1, V_DIM)), tok_spec, new_spec, new_spec, tok_spec]
                 + [page_spec(g) for g in range(g_pages)]
                 + [page_spec(g) for g in range(g_pages)],
        out_specs=tok_spec,
        scratch_shapes=[pltpu.VMEM((n_rows, D_ATTN), BF16),
                        pltpu.VMEM((n_rows, LANES), F32),
                        pltpu.VMEM((n_rows, LANES), F32),
                        pltpu.VMEM((n_rows, D_ATTN), F32),
                        pltpu.VMEM((g_pages * PAGE_SIZE, D_ATTN), BF16),
                        pltpu.VMEM((g_pages * PAGE_SIZE, D_ATTN), BF16)],
    )
    return pl.pallas_call(
        functools.partial(_attn_sample_kernel, pages_per_step=g_pages, t_len=t_len,
                          lam_init=lam_init),
        out_shape=jax.ShapeDtypeStruct((n_tok, D_ATTN), F32),
        grid_spec=grid_spec,
        compiler_params=pltpu.CompilerParams(
            dimension_semantics=("arbitrary", "arbitrary"), vmem_limit_bytes=VMEM_LIMIT),
        name="attn_sample",
    )(pt_flat, lam_rows, subln_g, q, k_new, v_new, gate, *([ck] * g_pages), *([cv] * g_pages))


def _outproj_kernel(x_ref, mc_ref, ma_ref, w_ref, g_ref, y_ref, *, final_norm):
    acc = jnp.dot(mc_ref[...].astype(BF16), w_ref[0:D_CONV, :], preferred_element_type=F32)
    acc = acc + jnp.dot(ma_ref[...].astype(BF16), w_ref[D_CONV:D_CONV + D_ATTN, :],
                        preferred_element_type=F32)
    y = x_ref[...] + acc
    y_ref[...] = _rmsnorm(y, g_ref[...]) if final_norm else y


def _outproj(x2d, mixc, mixa, w_bf, ln_f_g, final_norm):
    n_tok = x2d.shape[0]
    tm = 512
    tok_spec = lambda width: pl.BlockSpec((tm, width), lambda t: (t, 0))
    full = lambda shape: pl.BlockSpec(shape, lambda t: (0,) * len(shape))
    return pl.pallas_call(
        functools.partial(_outproj_kernel, final_norm=final_norm),
        out_shape=jax.ShapeDtypeStruct((n_tok, D_MODEL), F32),
        grid=(n_tok // tm,),
        in_specs=[tok_spec(D_MODEL), tok_spec(D_CONV), tok_spec(D_ATTN),
                  full((D_CONV + D_ATTN, D_MODEL)), full((1, D_MODEL))],
        out_specs=tok_spec(D_MODEL),
        compiler_params=pltpu.CompilerParams(dimension_semantics=("arbitrary",),
                                             vmem_limit_bytes=VMEM_LIMIT),
        name="outproj",
    )(x2d, mixc, mixa, w_bf, ln_f_g)


def kernel(x_prompt, x_sample, cache_k, cache_v, state_conv, page_table, ln_in_g, w_in, conv_w,
           lambda_q1, lambda_k1, lambda_q2, lambda_k2, subln_g, w_out, ln_f_g):
    depth = w_in.shape[0]
    n_p, s_p, _ = x_prompt.shape
    n_s, t_s, _ = x_sample.shape
    xp = x_prompt.reshape(n_p * s_p, D_MODEL)
    xs = x_sample.reshape(n_s * t_s, D_MODEL)
    ln_f = ln_f_g.reshape(1, D_MODEL)
    kp_l, vp_l, cp_l, ks_l, vs_l, cs_l = [], [], [], [], [], []
    for layer in range(depth):
        lam_init = _lambda_init(layer)
        last = layer == depth - 1
        ln_g = ln_in_g[layer].reshape(1, D_MODEL)
        w_in_bf = w_in[layer].astype(BF16)
        w_out_bf = w_out[layer].astype(BF16)
        sg = subln_g[layer].reshape(1, V_DIM)
        lam_rows = jnp.stack([lambda_q1[layer], lambda_k1[layer],
                              lambda_q2[layer], lambda_k2[layer]])

        q_p, k_p, v_p, kb_p, vb_p, mixc_p, gate_p, conv_p = _inproj_prompt(
            xp, ln_g, w_in_bf, conv_w[layer], n_p, s_p)
        mixa_p = _attn_prompt(lam_rows, sg, q_p, kb_p, vb_p, gate_p, n_p, s_p, lam_init)
        xp = _outproj(xp, mixc_p, mixa_p, w_out_bf, ln_f, last)
        kp_l.append(k_p.reshape(n_p, s_p // PAGE_SIZE, PAGE_SIZE, N_HEADS, 2 * HEAD_DIM))
        vp_l.append(v_p.reshape(n_p, s_p // PAGE_SIZE, PAGE_SIZE, N_HEADS, V_DIM))
        cp_l.append(conv_p)

        q_s, k_s, v_s, mixc_s, gate_s, conv_s = _inproj_sample(
            xs, ln_g, w_in_bf, conv_w[layer], state_conv[layer], n_s, t_s)
        mixa_s = _attn_sample(page_table, lam_rows, sg, q_s, k_s, v_s, gate_s,
                              cache_k, cache_v, layer, n_s, t_s, lam_init)
        xs = _outproj(xs, mixc_s, mixa_s, w_out_bf, ln_f, last)
        ks_l.append(k_s.reshape(n_s, t_s, N_HEADS, 2 * HEAD_DIM))
        vs_l.append(v_s.reshape(n_s, t_s, N_HEADS, V_DIM))
        cs_l.append(conv_s)

    y_prompt = xp.reshape(n_p, s_p, D_MODEL)
    y_sample = xs.reshape(n_s, t_s, D_MODEL)
    return (y_prompt, y_sample, jnp.stack(kp_l), jnp.stack(vp_l), jnp.stack(cp_l),
            jnp.stack(ks_l), jnp.stack(vs_l), jnp.stack(cs_l))
```

```python
import functools
import math

import jax
import jax.numpy as jnp
from jax import lax
from jax.experimental import pallas as pl
from jax.experimental.pallas import tpu as pltpu

D_MODEL = 1024
D_CONV = 512
D_ATTN = 512
N_HEADS = 4
HEAD_DIM = 64
V_DIM = 2 * HEAD_DIM
CONV_W = 3
PAGE_SIZE = 128
EPS = 1e-6
GROUP_W = 512
N_GROUPS = 8
SUBLANES = 8
LANES = 128
NEG_BIG = -1e30
VMEM_LIMIT = 56 * 1024 * 1024

F32 = jnp.float32
BF16 = jnp.bfloat16


def _lambda_init(layer):
    return 0.8 - 0.6 * math.exp(-0.3 * layer)


def _silu(z):
    return z / (1.0 + jnp.exp(-z))


def _rmsnorm(x, g):
    ms = jnp.mean(x * x, axis=-1, keepdims=True)
    return x * lax.rsqrt(ms + EPS) * g


def _proj(hb, w_ref, group):
    w = w_ref[:, group * GROUP_W:(group + 1) * GROUP_W]
    return jnp.dot(hb, w, preferred_element_type=F32)


def _diff_lambda(lam_ref, lam_init):
    lv = lam_ref[...]
    a = jnp.sum(lv[0:1] * lv[1:2], axis=-1, keepdims=True)
    b = jnp.sum(lv[2:3] * lv[3:4], axis=-1, keepdims=True)
    return jnp.exp(a) - jnp.exp(b) + lam_init


def _online_softmax(s, v_bf, m, l, acc):
    m_new = jnp.maximum(m, jnp.max(s, axis=-1, keepdims=True))
    alpha = jnp.exp(m - m_new)
    p = jnp.exp(s - pltpu.repeat(m_new, s.shape[1] // LANES, axis=1))
    l = alpha * l + jnp.sum(p, axis=-1, keepdims=True)
    pv = jnp.dot(p.astype(BF16), v_bf, preferred_element_type=F32)
    acc = pltpu.repeat(alpha, acc.shape[1] // LANES, axis=1) * acc + pv
    return m_new, l, acc


def _online_softmax_step(s, v_bf, m_ref, l_ref, acc_ref):
    m_ref[...], l_ref[...], acc_ref[...] = _online_softmax(
        s, v_bf, m_ref[...], l_ref[...], acc_ref[...])


def _store_head_rows(ref, x, n_tok):
    for h in range(N_HEADS):
        ref[pl.ds(h, n_tok, stride=N_HEADS), :] = x[:, h * V_DIM:(h + 1) * V_DIM]


def _load_head_rows(ref, n_tok):
    return jnp.concatenate(
        [ref[pl.ds(h, n_tok, stride=N_HEADS), :] for h in range(N_HEADS)], axis=1)


def _subln_gate(o, sg, gate, lam_init):
    ms = jnp.mean(o * o, axis=-1, keepdims=True)
    return gate * (o * lax.rsqrt(ms + EPS) * sg * (1.0 - lam_init))


def _inproj_prompt_kernel(x_ref, g_ref, w_ref, cw_ref,
                          q_ref, k_ref, v_ref, kb_ref, vb_ref, mixc_ref, gate_ref, conv_ref,
                          ubuf, *, tiles_per_seq, tm):
    t = pl.program_id(0)
    hb = _rmsnorm(x_ref[...], g_ref[...]).astype(BF16)

    @pl.when(t % tiles_per_seq == 0)
    def _():
        ubuf[0:SUBLANES, :] = jnp.zeros((SUBLANES, D_CONV), F32)

    u = _proj(hb, w_ref, 1) * _proj(hb, w_ref, 2)
    ubuf[SUBLANES:SUBLANES + tm, :] = u
    cw = cw_ref[...]
    y = (cw[0:1] * ubuf[SUBLANES - 2:SUBLANES - 2 + tm, :]
         + cw[1:2] * ubuf[SUBLANES - 1:SUBLANES - 1 + tm, :]
         + cw[2:3] * u)
    yc = _proj(hb, w_ref, 0) * y
    mixc_ref[...] = (_silu(_proj(hb, w_ref, 3)) * yc).astype(BF16)
    ubuf[0:SUBLANES, :] = u[tm - SUBLANES:tm, :]

    @pl.when(t % tiles_per_seq == tiles_per_seq - 1)
    def _():
        conv_ref[0] = u[tm - (CONV_W - 1):tm, :]

    q_ref[...] = (_proj(hb, w_ref, 4) * (HEAD_DIM ** -0.5)).astype(BF16)
    k = _proj(hb, w_ref, 5)
    _store_head_rows(k_ref, k, tm)
    kb_ref[...] = k.astype(BF16)
    v = _proj(hb, w_ref, 6)
    _store_head_rows(v_ref, v, tm)
    vb_ref[...] = v.astype(BF16)
    gate_ref[...] = _silu(_proj(hb, w_ref, 7)).astype(BF16)


def _inproj_prompt(x2d, ln_g, w_bf, conv_w, n_seq, seq_len):
    n_tok = x2d.shape[0]
    tm = 512
    tiles_per_seq = seq_len // tm
    n_tiles = n_tok // tm
    tok_spec = lambda width: pl.BlockSpec((tm, width), lambda t: (t, 0))
    head_rows = pl.BlockSpec((tm * N_HEADS, V_DIM), lambda t: (t, 0))
    full = lambda shape: pl.BlockSpec(shape, lambda t: (0,) * len(shape))
    out_shape = (
        jax.ShapeDtypeStruct((n_tok, D_ATTN), BF16),
        jax.ShapeDtypeStruct((n_tok * N_HEADS, V_DIM), F32),
        jax.ShapeDtypeStruct((n_tok * N_HEADS, V_DIM), F32),
        jax.ShapeDtypeStruct((n_tok, D_ATTN), BF16),
        jax.ShapeDtypeStruct((n_tok, D_ATTN), BF16),
        jax.ShapeDtypeStruct((n_tok, D_CONV), BF16),
        jax.ShapeDtypeStruct((n_tok, D_ATTN), BF16),
        jax.ShapeDtypeStruct((n_seq, CONV_W - 1, D_CONV), F32),
    )
    return pl.pallas_call(
        functools.partial(_inproj_prompt_kernel, tiles_per_seq=tiles_per_seq, tm=tm),
        out_shape=out_shape,
        grid=(n_tiles,),
        in_specs=[tok_spec(D_MODEL), full((1, D_MODEL)), full((D_MODEL, N_GROUPS * GROUP_W)),
                  full((CONV_W, D_CONV))],
        out_specs=(tok_spec(D_ATTN), head_rows, head_rows) + (tok_spec(D_ATTN),) * 4 + (
            pl.BlockSpec((1, CONV_W - 1, D_CONV), lambda t: (t // tiles_per_seq, 0, 0)),),
        scratch_shapes=[pltpu.VMEM((tm + SUBLANES, D_CONV), F32)],
        compiler_params=pltpu.CompilerParams(dimension_semantics=("arbitrary",),
                                             vmem_limit_bytes=VMEM_LIMIT),
        name="inproj_prompt",
    )(x2d, ln_g, w_bf, conv_w)


def _inproj_sample_kernel(x_ref, g_ref, w_ref, cw_ref, buf_ref,
                          q_ref, k_ref, v_ref, mixc_ref, gate_ref, conv_ref, *, n_seq, t_len):
    hb = _rmsnorm(x_ref[...], g_ref[...]).astype(BF16)
    n_tok = n_seq * t_len
    u3 = (_proj(hb, w_ref, 1) * _proj(hb, w_ref, 2)).reshape(n_seq, t_len, D_CONV)
    tpos = lax.broadcasted_iota(jnp.int32, (n_seq, t_len, D_CONV), 1)
    b0 = buf_ref[:, 0:1, :]
    b1 = buf_ref[:, 1:2, :]
    prev1 = jnp.where(tpos >= 1, pltpu.roll(u3, 1, 1), b1)
    prev2 = jnp.where(tpos >= 2, pltpu.roll(u3, 2, 1), jnp.where(tpos == 1, b1, b0))
    cw = cw_ref[...]
    y = cw[0:1][None] * prev2 + cw[1:2][None] * prev1 + cw[2:3][None] * u3
    yc = _proj(hb, w_ref, 0) * y.reshape(n_tok, D_CONV)
    mixc_ref[...] = _silu(_proj(hb, w_ref, 3)) * yc
    conv_ref[...] = u3[:, t_len - (CONV_W - 1):t_len, :]
    q_ref[...] = _proj(hb, w_ref, 4) * (HEAD_DIM ** -0.5)
    _store_head_rows(k_ref, _proj(hb, w_ref, 5), n_tok)
    _store_head_rows(v_ref, _proj(hb, w_ref, 6), n_tok)
    gate_ref[...] = _silu(_proj(hb, w_ref, 7))


def _inproj_sample(x2d, ln_g, w_bf, conv_w, buf, n_seq, t_len):
    assert t_len == SUBLANES
    n_tok = x2d.shape[0]
    tok = jax.ShapeDtypeStruct((n_tok, D_ATTN), F32)
    head_rows = jax.ShapeDtypeStruct((n_tok * N_HEADS, V_DIM), F32)
    return pl.pallas_call(
        functools.partial(_inproj_sample_kernel, n_seq=n_seq, t_len=t_len),
        out_shape=(tok, head_rows, head_rows, tok, tok,
                   jax.ShapeDtypeStruct((n_seq, CONV_W - 1, D_CONV), F32)),
        compiler_params=pltpu.CompilerParams(vmem_limit_bytes=VMEM_LIMIT),
        name="inproj_sample",
    )(x2d, ln_g, w_bf, conv_w, buf)


def _attn_prompt_kernel(lam_ref, sg_ref, q_ref, k_ref, v_ref, gate_ref, o_ref,
                        qs_ref, m_ref, l_ref, acc_ref, *, tq, tk, unroll, lam_init):
    qi = pl.program_id(2)
    q_start = qi * tq
    q = q_ref[...]
    lane = lax.broadcasted_iota(jnp.int32, (tq, V_DIM), 1)
    zero = jnp.zeros_like(q)
    qs_ref[0:tq, :] = jnp.where(lane < HEAD_DIM, q, zero)
    qs_ref[tq:2 * tq, :] = jnp.where(lane >= HEAD_DIM, q, zero)
    m_ref[...] = jnp.full(m_ref.shape, NEG_BIG, F32)
    l_ref[...] = jnp.zeros(l_ref.shape, F32)
    acc_ref[...] = jnp.zeros(acc_ref.shape, F32)

    def scores(ki):
        k_off = pl.multiple_of(ki * tk, tk)
        kblk = k_ref[pl.ds(k_off, tk), :]
        vblk = v_ref[pl.ds(k_off, tk), :]
        s = lax.dot_general(qs_ref[...], kblk, (((1,), (1,)), ((), ())),
                            preferred_element_type=F32)
        return s, vblk

    def full_tiles(first, count):
        state = (m_ref[...], l_ref[...], acc_ref[...])
        for u in range(count):
            s, vblk = scores(first + u)
            state = _online_softmax(s, vblk, *state)
        m_ref[...], l_ref[...], acc_ref[...] = state

    def group(gi, carry):
        full_tiles(gi * unroll, unroll)
        return carry

    n_full = q_start // tk
    n_groups = n_full // unroll
    lax.fori_loop(0, n_groups, group, 0)
    done = n_groups * unroll
    part = unroll // 2
    while part >= 1:
        take = (n_full - done) & part

        @pl.when(take != 0)
        def _(done=done, part=part):
            full_tiles(done, part)

        done = done + take
        part //= 2

    s, vblk = scores(n_full)
    row = lax.broadcasted_iota(jnp.int32, (2 * tq, tk), 0)
    col = lax.broadcasted_iota(jnp.int32, (2 * tq, tk), 1)
    q_pos = q_start + jnp.where(row >= tq, row - tq, row)
    s = jnp.where(n_full * tk + col <= q_pos, s, NEG_BIG)
    _online_softmax_step(s, vblk, m_ref, l_ref, acc_ref)

    acc = acc_ref[...]
    l = l_ref[...]
    o = acc[0:tq] / l[0:tq] - _diff_lambda(lam_ref, lam_init) * (acc[tq:2 * tq] / l[tq:2 * tq])
    o_ref[...] = _subln_gate(o, sg_ref[...], gate_ref[...].astype(F32), lam_init).astype(BF16)


def _attn_sample_step(lam_ref, sg_ref, q_ref, kn_ref, vn_ref, gate_ref, k_refs, v_refs, o_ref,
                      qbd_ref, m_ref, l_ref, acc_ref, kbuf, vbuf, *, j, n_steps, t_len, lam_init):
    g_pages = len(k_refs)
    n_rows = N_HEADS * 2 * t_len

    @pl.when(j == 0)
    def _():
        qt = jnp.concatenate([q_ref[...]] * (N_HEADS * 2), axis=0)
        row = lax.broadcasted_iota(jnp.int32, (n_rows, D_ATTN), 0)
        col = lax.broadcasted_iota(jnp.int32, (n_rows, D_ATTN), 1)
        keep = (col // HEAD_DIM) == (row // t_len)
        qbd_ref[...] = jnp.where(keep, qt, 0.0).astype(BF16)
        m_ref[...] = jnp.full(m_ref.shape, NEG_BIG, F32)
        l_ref[...] = jnp.zeros(l_ref.shape, F32)
        acc_ref[...] = jnp.zeros(acc_ref.shape, F32)

    for g in range(g_pages):
        rows = slice(g * PAGE_SIZE, (g + 1) * PAGE_SIZE)
        kbuf[rows, :] = _load_head_rows(k_refs[g], PAGE_SIZE).astype(BF16)
        vbuf[rows, :] = _load_head_rows(v_refs[g], PAGE_SIZE).astype(BF16)
    s = lax.dot_general(qbd_ref[...], kbuf[...], (((1,), (1,)), ((), ())),
                        preferred_element_type=F32)
    _online_softmax_step(s, vbuf[...], m_ref, l_ref, acc_ref)

    @pl.when(j == n_steps - 1)
    def _():
        pad = jnp.zeros((LANES - t_len, D_ATTN), F32)
        kn = jnp.concatenate([_load_head_rows(kn_ref, t_len), pad], axis=0).astype(BF16)
        vn = jnp.concatenate([_load_head_rows(vn_ref, t_len), pad], axis=0).astype(BF16)
        sn = lax.dot_general(qbd_ref[...], kn, (((1,), (1,)), ((), ())),
                             preferred_element_type=F32)
        row = lax.broadcasted_iota(jnp.int32, (n_rows, LANES), 0)
        col = lax.broadcasted_iota(jnp.int32, (n_rows, LANES), 1)
        sn = jnp.where(col <= row % t_len, sn, NEG_BIG)
        _online_softmax_step(sn, vn, m_ref, l_ref, acc_ref)

        lam = _diff_lambda(lam_ref, lam_init)
        acc = acc_ref[...]
        l = l_ref[...]
        sg = sg_ref[...]
        gate = gate_ref[...]
        for h in range(N_HEADS):
            r1 = (2 * h) * t_len
            r2 = (2 * h + 1) * t_len
            lanes = slice(h * V_DIM, (h + 1) * V_DIM)
            o = (acc[r1:r1 + t_len, lanes] / l[r1:r1 + t_len]
                 - lam * (acc[r2:r2 + t_len, lanes] / l[r2:r2 + t_len]))
            o_ref[:, lanes] = _subln_gate(o, sg, gate[:, lanes], lam_init)


def _attn_kernel(pt_ref, lam_ref, sg_ref, q_ref, k_ref, v_ref, gate_ref,
                 qs_ref, kn_ref, vn_ref, gate_s_ref, *rest,
                 g_pages, n_steps, nq, tq, tk, unroll, t_len, lam_init):
    del pt_ref
    k_pages = rest[:g_pages]
    v_pages = rest[g_pages:2 * g_pages]
    o_ref, os_ref = rest[2 * g_pages:2 * g_pages + 2]
    p_scratch = rest[2 * g_pages + 2:2 * g_pages + 6]
    s_scratch = rest[2 * g_pages + 6:]
    step = (pl.program_id(0) * N_HEADS + pl.program_id(1)) * nq + pl.program_id(2)
    _attn_sample_step(lam_ref, sg_ref, qs_ref, kn_ref, vn_ref, gate_s_ref, k_pages, v_pages,
                      os_ref, *s_scratch, j=step % n_steps, n_steps=n_steps, t_len=t_len,
                      lam_init=lam_init)
    _attn_prompt_kernel(lam_ref, sg_ref, q_ref, k_ref, v_ref, gate_ref, o_ref, *p_scratch,
                        tq=tq, tk=tk, unroll=unroll, lam_init=lam_init)


def _attention(page_table, lam_rows, subln_g, q_p, k_p, v_p, gate_p, n_seq_p, seq_len,
               q_s, k_new, v_new, gate_s, cache_k, cache_v, layer, n_seq_s, t_len, lam_init):
    assert t_len == SUBLANES and V_DIM == LANES
    tq, tk, unroll = 256, 512, 4
    assert tk % tq == 0 and seq_len % tk == 0 and unroll & (unroll - 1) == 0
    nq = seq_len // tq
    grid = (n_seq_p, N_HEADS, nq)
    n_grid = n_seq_p * N_HEADS * nq
    n_pages = page_table.shape[1]
    assert (n_seq_s * n_pages) % n_grid == 0
    g_pages = n_seq_s * n_pages // n_grid
    assert n_pages % g_pages == 0
    n_steps = n_pages // g_pages
    n_rows = N_HEADS * 2 * t_len
    pt_flat = page_table.reshape(-1)
    n_phys = cache_k.shape[1]
    page_rows = PAGE_SIZE * N_HEADS
    ck = cache_k.reshape(-1, V_DIM)
    cv = cache_v.reshape(-1, V_DIM)

    def sample_pos(b, h, i):
        step = (b * N_HEADS + h) * nq + i
        return step // n_steps, step % n_steps

    def page_spec(g):
        def index(b, h, i, pt):
            seq, j = sample_pos(b, h, i)
            return layer * n_phys + pt[seq * n_pages + j * g_pages + g], 0
        return pl.BlockSpec((page_rows, V_DIM), index)

    q_spec = pl.BlockSpec((tq, V_DIM), lambda b, h, i, pt: (b * nq + i, h))
    kv_spec = pl.BlockSpec((seq_len, V_DIM), lambda b, h, i, pt: (b, h))
    tok_spec = pl.BlockSpec((t_len, D_ATTN), lambda b, h, i, pt: (sample_pos(b, h, i)[0], 0))
    new_spec = pl.BlockSpec((t_len * N_HEADS, V_DIM),
                            lambda b, h, i, pt: (sample_pos(b, h, i)[0], 0))
    full = lambda shape: pl.BlockSpec(shape, lambda b, h, i, pt: (0,) * len(shape))
    grid_spec = pltpu.PrefetchScalarGridSpec(
        num_scalar_prefetch=1,
        grid=grid,
        in_specs=[full((4, HEAD_DIM)), full((1, V_DIM)), q_spec, kv_spec, kv_spec, q_spec,
                  tok_spec, new_spec, new_spec, tok_spec]
                 + [page_spec(g) for g in range(g_pages)]
                 + [page_spec(g) for g in range(g_pages)],
        out_specs=(q_spec, tok_spec),
        scratch_shapes=[pltpu.VMEM((2 * tq, V_DIM), BF16),
                        pltpu.VMEM((2 * tq, LANES), F32),
                        pltpu.VMEM((2 * tq, LANES), F32),
                        pltpu.VMEM((2 * tq, V_DIM), F32),
                        pltpu.VMEM((n_rows, D_ATTN), BF16),
                        pltpu.VMEM((n_rows, LANES), F32),
                        pltpu.VMEM((n_rows, LANES), F32),
                        pltpu.VMEM((n_rows, D_ATTN), F32),
                        pltpu.VMEM((g_pages * PAGE_SIZE, D_ATTN), BF16),
                        pltpu.VMEM((g_pages * PAGE_SIZE, D_ATTN), BF16)],
    )
    return pl.pallas_call(
        functools.partial(_attn_kernel, g_pages=g_pages, n_steps=n_steps, nq=nq, tq=tq, tk=tk,
                          unroll=unroll, t_len=t_len, lam_init=lam_init),
        out_shape=(jax.ShapeDtypeStruct((q_p.shape[0], D_ATTN), BF16),
                   jax.ShapeDtypeStruct((n_seq_s * t_len, D_ATTN), F32)),
        grid_spec=grid_spec,
        compiler_params=pltpu.CompilerParams(
            dimension_semantics=("arbitrary", "arbitrary", "arbitrary"),
            vmem_limit_bytes=VMEM_LIMIT),
        name="attention",
    )(pt_flat, lam_rows, subln_g, q_p, k_p, v_p, gate_p, q_s, k_new, v_new, gate_s,
      *([ck] * g_pages), *([cv] * g_pages))


def _outproj_kernel(x_ref, mc_ref, ma_ref, w_ref, g_ref, y_ref, *, final_norm):
    acc = jnp.dot(mc_ref[...].astype(BF16), w_ref[0:D_CONV, :], preferred_element_type=F32)
    acc = acc + jnp.dot(ma_ref[...].astype(BF16), w_ref[D_CONV:D_CONV + D_ATTN, :],
                        preferred_element_type=F32)
    y = x_ref[...] + acc
    y_ref[...] = _rmsnorm(y, g_ref[...]) if final_norm else y


def _outproj(x2d, mixc, mixa, w_bf, ln_f_g, final_norm):
    n_tok = x2d.shape[0]
    tm = 512
    tok_spec = lambda width: pl.BlockSpec((tm, width), lambda t: (t, 0))
    full = lambda shape: pl.BlockSpec(shape, lambda t: (0,) * len(shape))
    return pl.pallas_call(
        functools.partial(_outproj_kernel, final_norm=final_norm),
        out_shape=jax.ShapeDtypeStruct((n_tok, D_MODEL), F32),
        grid=(n_tok // tm,),
        in_specs=[tok_spec(D_MODEL), tok_spec(D_CONV), tok_spec(D_ATTN),
                  full((D_CONV + D_ATTN, D_MODEL)), full((1, D_MODEL))],
        out_specs=tok_spec(D_MODEL),
        compiler_params=pltpu.CompilerParams(dimension_semantics=("arbitrary",),
                                             vmem_limit_bytes=VMEM_LIMIT),
        name="outproj",
    )(x2d, mixc, mixa, w_bf, ln_f_g)


def kernel(x_prompt, x_sample, cache_k, cache_v, state_conv, page_table, ln_in_g, w_in, conv_w,
           lambda_q1, lambda_k1, lambda_q2, lambda_k2, subln_g, w_out, ln_f_g):
    depth = w_in.shape[0]
    n_p, s_p, _ = x_prompt.shape
    n_s, t_s, _ = x_sample.shape
    xp = x_prompt.reshape(n_p * s_p, D_MODEL)
    xs = x_sample.reshape(n_s * t_s, D_MODEL)
    ln_f = ln_f_g.reshape(1, D_MODEL)
    kp_l, vp_l, cp_l, ks_l, vs_l, cs_l = [], [], [], [], [], []
    for layer in range(depth):
        lam_init = _lambda_init(layer)
        last = layer == depth - 1
        ln_g = ln_in_g[layer].reshape(1, D_MODEL)
        w_in_bf = w_in[layer].astype(BF16)
        w_out_bf = w_out[layer].astype(BF16)
        sg = subln_g[layer].reshape(1, V_DIM)
        lam_rows = jnp.stack([lambda_q1[layer], lambda_k1[layer],
                              lambda_q2[layer], lambda_k2[layer]])

        q_p, k_p, v_p, kb_p, vb_p, mixc_p, gate_p, conv_p = _inproj_prompt(
            xp, ln_g, w_in_bf, conv_w[layer], n_p, s_p)
        q_s, k_s, v_s, mixc_s, gate_s, conv_s = _inproj_sample(
            xs, ln_g, w_in_bf, conv_w[layer], state_conv[layer], n_s, t_s)
        mixa_p, mixa_s = _attention(
            page_table, lam_rows, sg, q_p, kb_p, vb_p, gate_p, n_p, s_p,
            q_s, k_s, v_s, gate_s, cache_k, cache_v, layer, n_s, t_s, lam_init)
        xp = _outproj(xp, mixc_p, mixa_p, w_out_bf, ln_f, last)
        xs = _outproj(xs, mixc_s, mixa_s, w_out_bf, ln_f, last)
        kp_l.append(k_p.reshape(n_p, s_p // PAGE_SIZE, PAGE_SIZE, N_HEADS, 2 * HEAD_DIM))
        vp_l.append(v_p.reshape(n_p, s_p // PAGE_SIZE, PAGE_SIZE, N_HEADS, V_DIM))
        cp_l.append(conv_p)
        ks_l.append(k_s.reshape(n_s, t_s, N_HEADS, 2 * HEAD_DIM))
        vs_l.append(v_s.reshape(n_s, t_s, N_HEADS, V_DIM))
        cs_l.append(conv_s)

    y_prompt = xp.reshape(n_p, s_p, D_MODEL)
    y_sample = xs.reshape(n_s, t_s, D_MODEL)
    return (y_prompt, y_sample, jnp.stack(kp_l), jnp.stack(vp_l), jnp.stack(cp_l),
            jnp.stack(ks_l), jnp.stack(vs_l), jnp.stack(cs_l))
```

```python
import functools
import math

import jax
import jax.numpy as jnp
from jax import lax
from jax.experimental import pallas as pl
from jax.experimental.pallas import tpu as pltpu

D_MODEL = 1024
D_CONV = 512
D_ATTN = 512
N_HEADS = 4
HEAD_DIM = 64
V_DIM = 2 * HEAD_DIM
CONV_W = 3
PAGE_SIZE = 128
EPS = 1e-6
GROUP_W = 512
N_GROUPS = 8
SUBLANES = 8
LANES = 128
NEG_BIG = -1e30
VMEM_LIMIT = 56 * 1024 * 1024

F32 = jnp.float32
BF16 = jnp.bfloat16


def _lambda_init(layer):
    return 0.8 - 0.6 * math.exp(-0.3 * layer)


def _silu(z):
    return z / (1.0 + jnp.exp(-z))


def _rmsnorm(x, g):
    ms = jnp.mean(x * x, axis=-1, keepdims=True)
    return x * lax.rsqrt(ms + EPS) * g


def _proj(hb, w_ref, group):
    w = w_ref[:, group * GROUP_W:(group + 1) * GROUP_W]
    return jnp.dot(hb, w, preferred_element_type=F32)


def _diff_lambda(lam_ref, lam_init):
    lv = lam_ref[...]
    a = jnp.sum(lv[0:1] * lv[1:2], axis=-1, keepdims=True)
    b = jnp.sum(lv[2:3] * lv[3:4], axis=-1, keepdims=True)
    return jnp.exp(a) - jnp.exp(b) + lam_init


def _online_softmax(s, v_bf, m, l, acc):
    m_new = jnp.maximum(m, jnp.max(s, axis=-1, keepdims=True))
    alpha = jnp.exp(m - m_new)
    p = jnp.exp(s - pltpu.repeat(m_new, s.shape[1] // LANES, axis=1))
    l = alpha * l + jnp.sum(p, axis=-1, keepdims=True)
    pv = jnp.dot(p.astype(BF16), v_bf, preferred_element_type=F32)
    acc = pltpu.repeat(alpha, acc.shape[1] // LANES, axis=1) * acc + pv
    return m_new, l, acc


def _online_softmax_step(s, v_bf, m_ref, l_ref, acc_ref):
    m_ref[...], l_ref[...], acc_ref[...] = _online_softmax(
        s, v_bf, m_ref[...], l_ref[...], acc_ref[...])


def _store_head_rows(ref, x, n_tok):
    for h in range(N_HEADS):
        ref[pl.ds(h, n_tok, stride=N_HEADS), :] = x[:, h * V_DIM:(h + 1) * V_DIM]


def _load_head_rows(ref, n_tok):
    return jnp.concatenate(
        [ref[pl.ds(h, n_tok, stride=N_HEADS), :] for h in range(N_HEADS)], axis=1)


def _subln_gate(o, sg, gate, lam_init):
    ms = jnp.mean(o * o, axis=-1, keepdims=True)
    return gate * (o * lax.rsqrt(ms + EPS) * sg * (1.0 - lam_init))


def _inproj_prompt_kernel(x_ref, g_ref, w_ref, cw_ref,
                          q_ref, k_ref, v_ref, kb_ref, vb_ref, mixc_ref, gate_ref, conv_ref,
                          ubuf, *, tiles_per_seq, tm):
    t = pl.program_id(0)
    hb = _rmsnorm(x_ref[...], g_ref[...]).astype(BF16)

    @pl.when(t % tiles_per_seq == 0)
    def _():
        ubuf[0:SUBLANES, :] = jnp.zeros((SUBLANES, D_CONV), F32)

    u = _proj(hb, w_ref, 1) * _proj(hb, w_ref, 2)
    ubuf[SUBLANES:SUBLANES + tm, :] = u
    cw = cw_ref[...]
    y = (cw[0:1] * ubuf[SUBLANES - 2:SUBLANES - 2 + tm, :]
         + cw[1:2] * ubuf[SUBLANES - 1:SUBLANES - 1 + tm, :]
         + cw[2:3] * u)
    yc = _proj(hb, w_ref, 0) * y
    mixc_ref[...] = (_silu(_proj(hb, w_ref, 3)) * yc).astype(BF16)
    ubuf[0:SUBLANES, :] = u[tm - SUBLANES:tm, :]

    @pl.when(t % tiles_per_seq == tiles_per_seq - 1)
    def _():
        conv_ref[0] = u[tm - (CONV_W - 1):tm, :]

    q_ref[...] = (_proj(hb, w_ref, 4) * (HEAD_DIM ** -0.5)).astype(BF16)
    k = _proj(hb, w_ref, 5)
    _store_head_rows(k_ref, k, tm)
    kb_ref[...] = k.astype(BF16)
    v = _proj(hb, w_ref, 6)
    _store_head_rows(v_ref, v, tm)
    vb_ref[...] = v.astype(BF16)
    gate_ref[...] = _silu(_proj(hb, w_ref, 7)).astype(BF16)


def _inproj_prompt(x2d, ln_g, w_bf, conv_w, n_seq, seq_len):
    n_tok = x2d.shape[0]
    tm = 1024
    tiles_per_seq = seq_len // tm
    n_tiles = n_tok // tm
    tok_spec = lambda width: pl.BlockSpec((tm, width), lambda t: (t, 0))
    head_rows = pl.BlockSpec((tm * N_HEADS, V_DIM), lambda t: (t, 0))
    full = lambda shape: pl.BlockSpec(shape, lambda t: (0,) * len(shape))
    out_shape = (
        jax.ShapeDtypeStruct((n_tok, D_ATTN), BF16),
        jax.ShapeDtypeStruct((n_tok * N_HEADS, V_DIM), F32),
        jax.ShapeDtypeStruct((n_tok * N_HEADS, V_DIM), F32),
        jax.ShapeDtypeStruct((n_tok, D_ATTN), BF16),
        jax.ShapeDtypeStruct((n_tok, D_ATTN), BF16),
        jax.ShapeDtypeStruct((n_tok, D_CONV), BF16),
        jax.ShapeDtypeStruct((n_tok, D_ATTN), BF16),
        jax.ShapeDtypeStruct((n_seq, CONV_W - 1, D_CONV), F32),
    )
    return pl.pallas_call(
        functools.partial(_inproj_prompt_kernel, tiles_per_seq=tiles_per_seq, tm=tm),
        out_shape=out_shape,
        grid=(n_tiles,),
        in_specs=[tok_spec(D_MODEL), full((1, D_MODEL)), full((D_MODEL, N_GROUPS * GROUP_W)),
                  full((CONV_W, D_CONV))],
        out_specs=(tok_spec(D_ATTN), head_rows, head_rows) + (tok_spec(D_ATTN),) * 4 + (
            pl.BlockSpec((1, CONV_W - 1, D_CONV), lambda t: (t // tiles_per_seq, 0, 0)),),
        scratch_shapes=[pltpu.VMEM((tm + SUBLANES, D_CONV), F32)],
        compiler_params=pltpu.CompilerParams(dimension_semantics=("arbitrary",),
                                             vmem_limit_bytes=VMEM_LIMIT),
        name="inproj_prompt",
    )(x2d, ln_g, w_bf, conv_w)


def _inproj_sample_kernel(x_ref, g_ref, w_ref, cw_ref, buf_ref,
                          q_ref, k_ref, v_ref, mixc_ref, gate_ref, conv_ref, *, n_seq, t_len):
    hb = _rmsnorm(x_ref[...], g_ref[...]).astype(BF16)
    n_tok = n_seq * t_len
    u3 = (_proj(hb, w_ref, 1) * _proj(hb, w_ref, 2)).reshape(n_seq, t_len, D_CONV)
    tpos = lax.broadcasted_iota(jnp.int32, (n_seq, t_len, D_CONV), 1)
    b0 = buf_ref[:, 0:1, :]
    b1 = buf_ref[:, 1:2, :]
    prev1 = jnp.where(tpos >= 1, pltpu.roll(u3, 1, 1), b1)
    prev2 = jnp.where(tpos >= 2, pltpu.roll(u3, 2, 1), jnp.where(tpos == 1, b1, b0))
    cw = cw_ref[...]
    y = cw[0:1][None] * prev2 + cw[1:2][None] * prev1 + cw[2:3][None] * u3
    yc = _proj(hb, w_ref, 0) * y.reshape(n_tok, D_CONV)
    mixc_ref[...] = _silu(_proj(hb, w_ref, 3)) * yc
    conv_ref[...] = u3[:, t_len - (CONV_W - 1):t_len, :]
    q_ref[...] = _proj(hb, w_ref, 4) * (HEAD_DIM ** -0.5)
    _store_head_rows(k_ref, _proj(hb, w_ref, 5), n_tok)
    _store_head_rows(v_ref, _proj(hb, w_ref, 6), n_tok)
    gate_ref[...] = _silu(_proj(hb, w_ref, 7))


def _inproj_sample(x2d, ln_g, w_bf, conv_w, buf, n_seq, t_len):
    assert t_len == SUBLANES
    n_tok = x2d.shape[0]
    tok = jax.ShapeDtypeStruct((n_tok, D_ATTN), F32)
    head_rows = jax.ShapeDtypeStruct((n_tok * N_HEADS, V_DIM), F32)
    return pl.pallas_call(
        functools.partial(_inproj_sample_kernel, n_seq=n_seq, t_len=t_len),
        out_shape=(tok, head_rows, head_rows, tok, tok,
                   jax.ShapeDtypeStruct((n_seq, CONV_W - 1, D_CONV), F32)),
        compiler_params=pltpu.CompilerParams(vmem_limit_bytes=VMEM_LIMIT),
        name="inproj_sample",
    )(x2d, ln_g, w_bf, conv_w, buf)


def _attn_prompt_kernel(lam_ref, sg_ref, q_ref, k_ref, v_ref, gate_ref, o_ref,
                        qs_ref, m_ref, l_ref, acc_ref, *, tq, tk, unroll, lam_init,
                        before_diagonal=None):
    qi = pl.program_id(2)
    q_start = qi * tq
    q = q_ref[...]
    lane = lax.broadcasted_iota(jnp.int32, (tq, V_DIM), 1)
    zero = jnp.zeros_like(q)
    qs_ref[0:tq, :] = jnp.where(lane < HEAD_DIM, q, zero)
    qs_ref[tq:2 * tq, :] = jnp.where(lane >= HEAD_DIM, q, zero)
    m_ref[...] = jnp.full(m_ref.shape, NEG_BIG, F32)
    l_ref[...] = jnp.zeros(l_ref.shape, F32)
    acc_ref[...] = jnp.zeros(acc_ref.shape, F32)

    def scores(ki):
        k_off = pl.multiple_of(ki * tk, tk)
        kblk = k_ref[pl.ds(k_off, tk), :]
        vblk = v_ref[pl.ds(k_off, tk), :]
        s = lax.dot_general(qs_ref[...], kblk, (((1,), (1,)), ((), ())),
                            preferred_element_type=F32)
        return s, vblk

    def full_tiles(first, count):
        state = (m_ref[...], l_ref[...], acc_ref[...])
        for u in range(count):
            s, vblk = scores(first + u)
            state = _online_softmax(s, vblk, *state)
        m_ref[...], l_ref[...], acc_ref[...] = state

    def group(gi, carry):
        full_tiles(gi * unroll, unroll)
        return carry

    n_full = q_start // tk
    n_groups = n_full // unroll
    lax.fori_loop(0, n_groups, group, 0)
    done = n_groups * unroll
    part = unroll // 2
    while part >= 1:
        take = (n_full - done) & part

        @pl.when(take != 0)
        def _(done=done, part=part):
            full_tiles(done, part)

        done = done + take
        part //= 2

    if before_diagonal is not None:
        before_diagonal()

    s, vblk = scores(n_full)
    row = lax.broadcasted_iota(jnp.int32, (2 * tq, tk), 0)
    col = lax.broadcasted_iota(jnp.int32, (2 * tq, tk), 1)
    q_pos = q_start + jnp.where(row >= tq, row - tq, row)
    s = jnp.where(n_full * tk + col <= q_pos, s, NEG_BIG)
    _, l, acc = _online_softmax(s, vblk, m_ref[...], l_ref[...], acc_ref[...])

    o = acc[0:tq] / l[0:tq] - _diff_lambda(lam_ref, lam_init) * (acc[tq:2 * tq] / l[tq:2 * tq])
    o_ref[...] = _subln_gate(o, sg_ref[...], gate_ref[...].astype(F32), lam_init).astype(BF16)


def _sample_init(q_ref, qbd_ref, m_ref, l_ref, acc_ref, *, t_len):
    n_rows = N_HEADS * 2 * t_len
    qt = jnp.concatenate([q_ref[...]] * (N_HEADS * 2), axis=0)
    row = lax.broadcasted_iota(jnp.int32, (n_rows, D_ATTN), 0)
    col = lax.broadcasted_iota(jnp.int32, (n_rows, D_ATTN), 1)
    keep = (col // HEAD_DIM) == (row // t_len)
    qbd_ref[...] = jnp.where(keep, qt, 0.0).astype(BF16)
    m_ref[...] = jnp.full(m_ref.shape, NEG_BIG, F32)
    l_ref[...] = jnp.zeros(l_ref.shape, F32)
    acc_ref[...] = jnp.zeros(acc_ref.shape, F32)


def _sample_pages(k_refs, v_refs, qbd_ref, m_ref, l_ref, acc_ref, kbuf, vbuf):
    for g in range(len(k_refs)):
        rows = slice(g * PAGE_SIZE, (g + 1) * PAGE_SIZE)
        kbuf[rows, :] = _load_head_rows(k_refs[g], PAGE_SIZE).astype(BF16)
        vbuf[rows, :] = _load_head_rows(v_refs[g], PAGE_SIZE).astype(BF16)
    s = lax.dot_general(qbd_ref[...], kbuf[...], (((1,), (1,)), ((), ())),
                        preferred_element_type=F32)
    _online_softmax_step(s, vbuf[...], m_ref, l_ref, acc_ref)


def _sample_finish(lam_ref, sg_ref, kn_ref, vn_ref, gate_ref, o_ref, qbd_ref, m_ref, l_ref,
                   acc_ref, *, t_len, lam_init):
    n_rows = N_HEADS * 2 * t_len
    pad = jnp.zeros((LANES - t_len, D_ATTN), F32)
    kn = jnp.concatenate([_load_head_rows(kn_ref, t_len), pad], axis=0).astype(BF16)
    vn = jnp.concatenate([_load_head_rows(vn_ref, t_len), pad], axis=0).astype(BF16)
    sn = lax.dot_general(qbd_ref[...], kn, (((1,), (1,)), ((), ())),
                         preferred_element_type=F32)
    row = lax.broadcasted_iota(jnp.int32, (n_rows, LANES), 0)
    col = lax.broadcasted_iota(jnp.int32, (n_rows, LANES), 1)
    sn = jnp.where(col <= row % t_len, sn, NEG_BIG)
    _, l, acc = _online_softmax(sn, vn, m_ref[...], l_ref[...], acc_ref[...])

    lam = _diff_lambda(lam_ref, lam_init)
    sg = sg_ref[...]
    gate = gate_ref[...]
    for h in range(N_HEADS):
        r1 = (2 * h) * t_len
        r2 = (2 * h + 1) * t_len
        lanes = slice(h * V_DIM, (h + 1) * V_DIM)
        o = (acc[r1:r1 + t_len, lanes] / l[r1:r1 + t_len]
             - lam * (acc[r2:r2 + t_len, lanes] / l[r2:r2 + t_len]))
        o_ref[:, lanes] = _subln_gate(o, sg, gate[:, lanes], lam_init)


def _attn_kernel(pt_ref, lam_ref, sg_ref, q_ref, k_ref, v_ref, gate_ref,
                 qs_ref, kn_ref, vn_ref, gate_s_ref, *rest,
                 g_pages, n_steps, nq, tq, tk, unroll, t_len, lam_init):
    del pt_ref
    k_pages = rest[:g_pages]
    v_pages = rest[g_pages:2 * g_pages]
    o_ref, os_ref = rest[2 * g_pages:2 * g_pages + 2]
    p_scratch = rest[2 * g_pages + 2:2 * g_pages + 6]
    qbd_ref, m_ref, l_ref, acc_ref, kbuf, vbuf = rest[2 * g_pages + 6:]
    step = (pl.program_id(0) * N_HEADS + pl.program_id(1)) * nq + pl.program_id(2)
    j = step % n_steps

    @pl.when(j == 0)
    def _():
        _sample_init(qs_ref, qbd_ref, m_ref, l_ref, acc_ref, t_len=t_len)

    _attn_prompt_kernel(
        lam_ref, sg_ref, q_ref, k_ref, v_ref, gate_ref, o_ref, *p_scratch,
        tq=tq, tk=tk, unroll=unroll, lam_init=lam_init,
        before_diagonal=functools.partial(_sample_pages, k_pages, v_pages, qbd_ref, m_ref, l_ref,
                                          acc_ref, kbuf, vbuf))

    @pl.when(j == n_steps - 1)
    def _():
        _sample_finish(lam_ref, sg_ref, kn_ref, vn_ref, gate_s_ref, os_ref, qbd_ref, m_ref, l_ref,
                       acc_ref, t_len=t_len, lam_init=lam_init)


def _attention(page_table, lam_rows, subln_g, q_p, k_p, v_p, gate_p, n_seq_p, seq_len,
               q_s, k_new, v_new, gate_s, cache_k, cache_v, layer, n_seq_s, t_len, lam_init):
    assert t_len == SUBLANES and V_DIM == LANES
    tq, tk, unroll = 256, 512, 4
    assert tk % tq == 0 and seq_len % tk == 0 and unroll & (unroll - 1) == 0
    nq = seq_len // tq
    grid = (n_seq_p, N_HEADS, nq)
    n_grid = n_seq_p * N_HEADS * nq
    n_pages = page_table.shape[1]
    assert (n_seq_s * n_pages) % n_grid == 0
    g_pages = n_seq_s * n_pages // n_grid
    assert n_pages % g_pages == 0
    n_steps = n_pages // g_pages
    n_rows = N_HEADS * 2 * t_len
    pt_flat = page_table.reshape(-1)
    n_phys = cache_k.shape[1]
    page_rows = PAGE_SIZE * N_HEADS
    ck = cache_k.reshape(-1, V_DIM)
    cv = cache_v.reshape(-1, V_DIM)

    def sample_pos(b, h, i):
        step = (b * N_HEADS + h) * nq + i
        return step // n_steps, step % n_steps

    def page_spec(g):
        def index(b, h, i, pt):
            seq, j = sample_pos(b, h, i)
            return layer * n_phys + pt[seq * n_pages + j * g_pages + g], 0
        return pl.BlockSpec((page_rows, V_DIM), index)

    q_spec = pl.BlockSpec((tq, V_DIM), lambda b, h, i, pt: (b * nq + i, h))
    kv_spec = pl.BlockSpec((seq_len, V_DIM), lambda b, h, i, pt: (b, h))
    tok_spec = pl.BlockSpec((t_len, D_ATTN), lambda b, h, i, pt: (sample_pos(b, h, i)[0], 0))
    new_spec = pl.BlockSpec((t_len * N_HEADS, V_DIM),
                            lambda b, h, i, pt: (sample_pos(b, h, i)[0], 0))
    full = lambda shape: pl.BlockSpec(shape, lambda b, h, i, pt: (0,) * len(shape))
    grid_spec = pltpu.PrefetchScalarGridSpec(
        num_scalar_prefetch=1,
        grid=grid,
        in_specs=[full((4, HEAD_DIM)), full((1, V_DIM)), q_spec, kv_spec, kv_spec, q_spec,
                  tok_spec, new_spec, new_spec, tok_spec]
                 + [page_spec(g) for g in range(g_pages)]
                 + [page_spec(g) for g in range(g_pages)],
        out_specs=(q_spec, tok_spec),
        scratch_shapes=[pltpu.VMEM((2 * tq, V_DIM), BF16),
                        pltpu.VMEM((2 * tq, LANES), F32),
                        pltpu.VMEM((2 * tq, LANES), F32),
                        pltpu.VMEM((2 * tq, V_DIM), F32),
                        pltpu.VMEM((n_rows, D_ATTN), BF16),
                        pltpu.VMEM((n_rows, LANES), F32),
                        pltpu.VMEM((n_rows, LANES), F32),
                        pltpu.VMEM((n_rows, D_ATTN), F32),
                        pltpu.VMEM((g_pages * PAGE_SIZE, D_ATTN), BF16),
                        pltpu.VMEM((g_pages * PAGE_SIZE, D_ATTN), BF16)],
    )
    return pl.pallas_call(
        functools.partial(_attn_kernel, g_pages=g_pages, n_steps=n_steps, nq=nq, tq=tq, tk=tk,
                          unroll=unroll, t_len=t_len, lam_init=lam_init),
        out_shape=(jax.ShapeDtypeStruct((q_p.shape[0], D_ATTN), BF16),
                   jax.ShapeDtypeStruct((n_seq_s * t_len, D_ATTN), F32)),
        grid_spec=grid_spec,
        compiler_params=pltpu.CompilerParams(
            dimension_semantics=("arbitrary", "arbitrary", "arbitrary"),
            vmem_limit_bytes=VMEM_LIMIT),
        name="attention",
    )(pt_flat, lam_rows, subln_g, q_p, k_p, v_p, gate_p, q_s, k_new, v_new, gate_s,
      *([ck] * g_pages), *([cv] * g_pages))


def _outproj_kernel(x_ref, mc_ref, ma_ref, w_ref, g_ref, y_ref, *, final_norm):
    acc = jnp.dot(mc_ref[...].astype(BF16), w_ref[0:D_CONV, :], preferred_element_type=F32)
    acc = acc + jnp.dot(ma_ref[...].astype(BF16), w_ref[D_CONV:D_CONV + D_ATTN, :],
                        preferred_element_type=F32)
    y = x_ref[...] + acc
    y_ref[...] = _rmsnorm(y, g_ref[...]) if final_norm else y


def _outproj(x2d, mixc, mixa, w_bf, ln_f_g, final_norm):
    n_tok = x2d.shape[0]
    tm = 1024
    tok_spec = lambda width: pl.BlockSpec((tm, width), lambda t: (t, 0))
    full = lambda shape: pl.BlockSpec(shape, lambda t: (0,) * len(shape))
    return pl.pallas_call(
        functools.partial(_outproj_kernel, final_norm=final_norm),
        out_shape=jax.ShapeDtypeStruct((n_tok, D_MODEL), F32),
        grid=(n_tok // tm,),
        in_specs=[tok_spec(D_MODEL), tok_spec(D_CONV), tok_spec(D_ATTN),
                  full((D_CONV + D_ATTN, D_MODEL)), full((1, D_MODEL))],
        out_specs=tok_spec(D_MODEL),
        compiler_params=pltpu.CompilerParams(dimension_semantics=("arbitrary",),
                                             vmem_limit_bytes=VMEM_LIMIT),
        name="outproj",
    )(x2d, mixc, mixa, w_bf, ln_f_g)


def kernel(x_prompt, x_sample, cache_k, cache_v, state_conv, page_table, ln_in_g, w_in, conv_w,
           lambda_q1, lambda_k1, lambda_q2, lambda_k2, subln_g, w_out, ln_f_g):
    depth = w_in.shape[0]
    n_p, s_p, _ = x_prompt.shape
    n_s, t_s, _ = x_sample.shape
    xp = x_prompt.reshape(n_p * s_p, D_MODEL)
    xs = x_sample.reshape(n_s * t_s, D_MODEL)
    ln_f = ln_f_g.reshape(1, D_MODEL)
    kp_l, vp_l, cp_l, ks_l, vs_l, cs_l = [], [], [], [], [], []
    for layer in range(depth):
        lam_init = _lambda_init(layer)
        last = layer == depth - 1
        ln_g = ln_in_g[layer].reshape(1, D_MODEL)
        w_in_bf = w_in[layer].astype(BF16)
        w_out_bf = w_out[layer].astype(BF16)
        sg = subln_g[layer].reshape(1, V_DIM)
        lam_rows = jnp.stack([lambda_q1[layer], lambda_k1[layer],
                              lambda_q2[layer], lambda_k2[layer]])

        q_p, k_p, v_p, kb_p, vb_p, mixc_p, gate_p, conv_p = _inproj_prompt(
            xp, ln_g, w_in_bf, conv_w[layer], n_p, s_p)
        q_s, k_s, v_s, mixc_s, gate_s, conv_s = _inproj_sample(
            xs, ln_g, w_in_bf, conv_w[layer], state_conv[layer], n_s, t_s)
        mixa_p, mixa_s = _attention(
            page_table, lam_rows, sg, q_p, kb_p, vb_p, gate_p, n_p, s_p,
            q_s, k_s, v_s, gate_s, cache_k, cache_v, layer, n_s, t_s, lam_init)
        xp = _outproj(xp, mixc_p, mixa_p, w_out_bf, ln_f, last)
        xs = _outproj(xs, mixc_s, mixa_s, w_out_bf, ln_f, last)
        kp_l.append(k_p.reshape(n_p, s_p // PAGE_SIZE, PAGE_SIZE, N_HEADS, 2 * HEAD_DIM))
        vp_l.append(v_p.reshape(n_p, s_p // PAGE_SIZE, PAGE_SIZE, N_HEADS, V_DIM))
        cp_l.append(conv_p)
        ks_l.append(k_s.reshape(n_s, t_s, N_HEADS, 2 * HEAD_DIM))
        vs_l.append(v_s.reshape(n_s, t_s, N_HEADS, V_DIM))
        cs_l.append(conv_s)

    y_prompt = xp.reshape(n_p, s_p, D_MODEL)
    y_sample = xs.reshape(n_s, t_s, D_MODEL)
    return (y_prompt, y_sample, jnp.stack(kp_l), jnp.stack(vp_l), jnp.stack(cp_l),
            jnp.stack(ks_l), jnp.stack(vs_l), jnp.stack(cs_l))
```

```python
import functools
import math

import jax
import jax.numpy as jnp
from jax import lax
from jax.experimental import pallas as pl
from jax.experimental.pallas import tpu as pltpu

D_MODEL = 1024
D_CONV = 512
D_ATTN = 512
N_HEADS = 4
HEAD_DIM = 64
V_DIM = 2 * HEAD_DIM
CONV_W = 3
PAGE_SIZE = 128
EPS = 1e-6
GROUP_W = 512
N_GROUPS = 8
SUBLANES = 8
LANES = 128
NEG_BIG = -1e30
VMEM_LIMIT = 56 * 1024 * 1024

F32 = jnp.float32
BF16 = jnp.bfloat16


def _lambda_init(layer):
    return 0.8 - 0.6 * math.exp(-0.3 * layer)


def _silu(z):
    return z / (1.0 + jnp.exp(-z))


def _rmsnorm(x, g):
    ms = jnp.mean(x * x, axis=-1, keepdims=True)
    return x * lax.rsqrt(ms + EPS) * g


def _proj(hb, w_ref, group):
    w = w_ref[:, group * GROUP_W:(group + 1) * GROUP_W]
    return jnp.dot(hb, w, preferred_element_type=F32)


def _diff_lambda(lam_ref, lam_init):
    lv = lam_ref[...]
    a = jnp.sum(lv[0:1] * lv[1:2], axis=-1, keepdims=True)
    b = jnp.sum(lv[2:3] * lv[3:4], axis=-1, keepdims=True)
    return jnp.exp(a) - jnp.exp(b) + lam_init


def _online_softmax(s, v_bf, m, l, acc):
    m_new = jnp.maximum(m, jnp.max(s, axis=-1, keepdims=True))
    alpha = jnp.exp(m - m_new)
    p = jnp.exp(s - pltpu.repeat(m_new, s.shape[1] // LANES, axis=1))
    l = alpha * l + jnp.sum(p, axis=-1, keepdims=True)
    pv = jnp.dot(p.astype(BF16), v_bf, preferred_element_type=F32)
    acc = pltpu.repeat(alpha, acc.shape[1] // LANES, axis=1) * acc + pv
    return m_new, l, acc


def _online_softmax_step(s, v_bf, m_ref, l_ref, acc_ref):
    m_ref[...], l_ref[...], acc_ref[...] = _online_softmax(
        s, v_bf, m_ref[...], l_ref[...], acc_ref[...])


def _store_head_rows(ref, x, n_tok):
    for h in range(N_HEADS):
        ref[pl.ds(h, n_tok, stride=N_HEADS), :] = x[:, h * V_DIM:(h + 1) * V_DIM]


def _load_head_rows(ref, n_tok):
    return jnp.concatenate(
        [ref[pl.ds(h, n_tok, stride=N_HEADS), :] for h in range(N_HEADS)], axis=1)


def _subln_gate(o, sg, gate, lam_init):
    ms = jnp.mean(o * o, axis=-1, keepdims=True)
    return gate * (o * lax.rsqrt(ms + EPS) * sg * (1.0 - lam_init))


def _inproj_prompt_kernel(x_ref, g_ref, w_ref, cw_ref,
                          q_ref, k_ref, v_ref, kb_ref, vb_ref, mixc_ref, gate_ref, conv_ref,
                          ubuf, *, tiles_per_seq, tm):
    t = pl.program_id(0)
    hb = _rmsnorm(x_ref[...], g_ref[...]).astype(BF16)

    @pl.when(t % tiles_per_seq == 0)
    def _():
        ubuf[0:SUBLANES, :] = jnp.zeros((SUBLANES, D_CONV), F32)

    u = _proj(hb, w_ref, 1) * _proj(hb, w_ref, 2)
    ubuf[SUBLANES:SUBLANES + tm, :] = u
    cw = cw_ref[...]
    y = (cw[0:1] * ubuf[SUBLANES - 2:SUBLANES - 2 + tm, :]
         + cw[1:2] * ubuf[SUBLANES - 1:SUBLANES - 1 + tm, :]
         + cw[2:3] * u)
    yc = _proj(hb, w_ref, 0) * y
    mixc_ref[...] = (_silu(_proj(hb, w_ref, 3)) * yc).astype(BF16)
    ubuf[0:SUBLANES, :] = u[tm - SUBLANES:tm, :]

    @pl.when(t % tiles_per_seq == tiles_per_seq - 1)
    def _():
        conv_ref[0] = u[tm - (CONV_W - 1):tm, :]

    q_ref[...] = (_proj(hb, w_ref, 4) * (HEAD_DIM ** -0.5)).astype(BF16)
    k = _proj(hb, w_ref, 5)
    _store_head_rows(k_ref, k, tm)
    kb_ref[...] = k.astype(BF16)
    v = _proj(hb, w_ref, 6)
    _store_head_rows(v_ref, v, tm)
    vb_ref[...] = v.astype(BF16)
    gate_ref[...] = _silu(_proj(hb, w_ref, 7)).astype(BF16)


def _inproj_prompt(x2d, ln_g, w_bf, conv_w, n_seq, seq_len):
    n_tok = x2d.shape[0]
    tm = 1024
    tiles_per_seq = seq_len // tm
    n_tiles = n_tok // tm
    tok_spec = lambda width: pl.BlockSpec((tm, width), lambda t: (t, 0))
    head_rows = pl.BlockSpec((tm * N_HEADS, V_DIM), lambda t: (t, 0))
    full = lambda shape: pl.BlockSpec(shape, lambda t: (0,) * len(shape))
    out_shape = (
        jax.ShapeDtypeStruct((n_tok, D_ATTN), BF16),
        jax.ShapeDtypeStruct((n_tok * N_HEADS, V_DIM), F32),
        jax.ShapeDtypeStruct((n_tok * N_HEADS, V_DIM), F32),
        jax.ShapeDtypeStruct((n_tok, D_ATTN), BF16),
        jax.ShapeDtypeStruct((n_tok, D_ATTN), BF16),
        jax.ShapeDtypeStruct((n_tok, D_CONV), BF16),
        jax.ShapeDtypeStruct((n_tok, D_ATTN), BF16),
        jax.ShapeDtypeStruct((n_seq, CONV_W - 1, D_CONV), F32),
    )
    return pl.pallas_call(
        functools.partial(_inproj_prompt_kernel, tiles_per_seq=tiles_per_seq, tm=tm),
        out_shape=out_shape,
        grid=(n_tiles,),
        in_specs=[tok_spec(D_MODEL), full((1, D_MODEL)), full((D_MODEL, N_GROUPS * GROUP_W)),
                  full((CONV_W, D_CONV))],
        out_specs=(tok_spec(D_ATTN), head_rows, head_rows) + (tok_spec(D_ATTN),) * 4 + (
            pl.BlockSpec((1, CONV_W - 1, D_CONV), lambda t: (t // tiles_per_seq, 0, 0)),),
        scratch_shapes=[pltpu.VMEM((tm + SUBLANES, D_CONV), F32)],
        compiler_params=pltpu.CompilerParams(dimension_semantics=("arbitrary",),
                                             vmem_limit_bytes=VMEM_LIMIT),
        name="inproj_prompt",
    )(x2d, ln_g, w_bf, conv_w)


def _inproj_sample_kernel(x_ref, g_ref, w_ref, cw_ref, buf_ref,
                          q_ref, k_ref, v_ref, mixc_ref, gate_ref, conv_ref, *, n_seq, t_len):
    hb = _rmsnorm(x_ref[...], g_ref[...]).astype(BF16)
    n_tok = n_seq * t_len
    u3 = (_proj(hb, w_ref, 1) * _proj(hb, w_ref, 2)).reshape(n_seq, t_len, D_CONV)
    tpos = lax.broadcasted_iota(jnp.int32, (n_seq, t_len, D_CONV), 1)
    b0 = buf_ref[:, 0:1, :]
    b1 = buf_ref[:, 1:2, :]
    prev1 = jnp.where(tpos >= 1, pltpu.roll(u3, 1, 1), b1)
    prev2 = jnp.where(tpos >= 2, pltpu.roll(u3, 2, 1), jnp.where(tpos == 1, b1, b0))
    cw = cw_ref[...]
    y = cw[0:1][None] * prev2 + cw[1:2][None] * prev1 + cw[2:3][None] * u3
    yc = _proj(hb, w_ref, 0) * y.reshape(n_tok, D_CONV)
    mixc_ref[...] = _silu(_proj(hb, w_ref, 3)) * yc
    conv_ref[...] = u3[:, t_len - (CONV_W - 1):t_len, :]
    q_ref[...] = _proj(hb, w_ref, 4) * (HEAD_DIM ** -0.5)
    _store_head_rows(k_ref, _proj(hb, w_ref, 5), n_tok)
    _store_head_rows(v_ref, _proj(hb, w_ref, 6), n_tok)
    gate_ref[...] = _silu(_proj(hb, w_ref, 7))


def _inproj_sample(x2d, ln_g, w_bf, conv_w, buf, n_seq, t_len):
    assert t_len == SUBLANES
    n_tok = x2d.shape[0]
    tok = jax.ShapeDtypeStruct((n_tok, D_ATTN), F32)
    head_rows = jax.ShapeDtypeStruct((n_tok * N_HEADS, V_DIM), F32)
    return pl.pallas_call(
        functools.partial(_inproj_sample_kernel, n_seq=n_seq, t_len=t_len),
        out_shape=(tok, head_rows, head_rows, tok, tok,
                   jax.ShapeDtypeStruct((n_seq, CONV_W - 1, D_CONV), F32)),
        compiler_params=pltpu.CompilerParams(vmem_limit_bytes=VMEM_LIMIT),
        name="inproj_sample",
    )(x2d, ln_g, w_bf, conv_w, buf)


def _attn_prompt_kernel(lam_ref, sg_ref, q_ref, k_ref, v_ref, gate_ref, o_ref,
                        qs_ref, m_ref, l_ref, acc_ref, *, tq, tk, unroll, lam_init,
                        before_diagonal=None):
    qi = pl.program_id(2)
    q_start = qi * tq
    q = q_ref[...]
    lane = lax.broadcasted_iota(jnp.int32, (tq, V_DIM), 1)
    zero = jnp.zeros_like(q)
    qs_ref[0:tq, :] = jnp.where(lane < HEAD_DIM, q, zero)
    qs_ref[tq:2 * tq, :] = jnp.where(lane >= HEAD_DIM, q, zero)
    m_ref[...] = jnp.full(m_ref.shape, NEG_BIG, F32)
    l_ref[...] = jnp.zeros(l_ref.shape, F32)
    acc_ref[...] = jnp.zeros(acc_ref.shape, F32)

    def scores(ki):
        k_off = pl.multiple_of(ki * tk, tk)
        kblk = k_ref[pl.ds(k_off, tk), :]
        vblk = v_ref[pl.ds(k_off, tk), :]
        s = lax.dot_general(qs_ref[...], kblk, (((1,), (1,)), ((), ())),
                            preferred_element_type=F32)
        return s, vblk

    def full_tiles(first, count):
        state = (m_ref[...], l_ref[...], acc_ref[...])
        for u in range(count):
            s, vblk = scores(first + u)
            state = _online_softmax(s, vblk, *state)
        m_ref[...], l_ref[...], acc_ref[...] = state

    def group(gi, carry):
        full_tiles(gi * unroll, unroll)
        return carry

    n_full = q_start // tk
    n_groups = n_full // unroll
    lax.fori_loop(0, n_groups, group, 0)
    done = n_groups * unroll
    part = unroll // 2
    while part >= 1:
        take = (n_full - done) & part

        @pl.when(take != 0)
        def _(done=done, part=part):
            full_tiles(done, part)

        done = done + take
        part //= 2

    if before_diagonal is not None:
        before_diagonal()

    s, vblk = scores(n_full)
    row = lax.broadcasted_iota(jnp.int32, (2 * tq, tk), 0)
    col = lax.broadcasted_iota(jnp.int32, (2 * tq, tk), 1)
    q_pos = q_start + jnp.where(row >= tq, row - tq, row)
    s = jnp.where(n_full * tk + col <= q_pos, s, NEG_BIG)
    _, l, acc = _online_softmax(s, vblk, m_ref[...], l_ref[...], acc_ref[...])

    o = acc[0:tq] / l[0:tq] - _diff_lambda(lam_ref, lam_init) * (acc[tq:2 * tq] / l[tq:2 * tq])
    o_ref[...] = _subln_gate(o, sg_ref[...], gate_ref[...].astype(F32), lam_init).astype(BF16)


def _sample_init(q_ref, qbd_ref, m_ref, l_ref, acc_ref, *, t_len):
    n_rows = N_HEADS * 2 * t_len
    qt = jnp.concatenate([q_ref[...]] * (N_HEADS * 2), axis=0)
    row = lax.broadcasted_iota(jnp.int32, (n_rows, D_ATTN), 0)
    col = lax.broadcasted_iota(jnp.int32, (n_rows, D_ATTN), 1)
    keep = (col // HEAD_DIM) == (row // t_len)
    qbd_ref[...] = jnp.where(keep, qt, 0.0).astype(BF16)
    m_ref[...] = jnp.full(m_ref.shape, NEG_BIG, F32)
    l_ref[...] = jnp.zeros(l_ref.shape, F32)
    acc_ref[...] = jnp.zeros(acc_ref.shape, F32)


def _sample_pages(k_refs, v_refs, qbd_ref, m_ref, l_ref, acc_ref, kbuf, vbuf):
    for g in range(len(k_refs)):
        rows = slice(g * PAGE_SIZE, (g + 1) * PAGE_SIZE)
        kbuf[rows, :] = _load_head_rows(k_refs[g], PAGE_SIZE).astype(BF16)
        vbuf[rows, :] = _load_head_rows(v_refs[g], PAGE_SIZE).astype(BF16)
    s = lax.dot_general(qbd_ref[...], kbuf[...], (((1,), (1,)), ((), ())),
                        preferred_element_type=F32)
    _online_softmax_step(s, vbuf[...], m_ref, l_ref, acc_ref)


def _sample_finish(lam_ref, sg_ref, kn_ref, vn_ref, gate_ref, o_ref, qbd_ref, m_ref, l_ref,
                   acc_ref, *, t_len, lam_init):
    n_rows = N_HEADS * 2 * t_len
    pad = jnp.zeros((LANES - t_len, D_ATTN), F32)
    kn = jnp.concatenate([_load_head_rows(kn_ref, t_len), pad], axis=0).astype(BF16)
    vn = jnp.concatenate([_load_head_rows(vn_ref, t_len), pad], axis=0).astype(BF16)
    sn = lax.dot_general(qbd_ref[...], kn, (((1,), (1,)), ((), ())),
                         preferred_element_type=F32)
    row = lax.broadcasted_iota(jnp.int32, (n_rows, LANES), 0)
    col = lax.broadcasted_iota(jnp.int32, (n_rows, LANES), 1)
    sn = jnp.where(col <= row % t_len, sn, NEG_BIG)
    _, l, acc = _online_softmax(sn, vn, m_ref[...], l_ref[...], acc_ref[...])

    lam = _diff_lambda(lam_ref, lam_init)
    sg = sg_ref[...]
    gate = gate_ref[...]
    for h in range(N_HEADS):
        r1 = (2 * h) * t_len
        r2 = (2 * h + 1) * t_len
        lanes = slice(h * V_DIM, (h + 1) * V_DIM)
        o = (acc[r1:r1 + t_len, lanes] / l[r1:r1 + t_len]
             - lam * (acc[r2:r2 + t_len, lanes] / l[r2:r2 + t_len]))
        o_ref[:, lanes] = _subln_gate(o, sg, gate[:, lanes], lam_init)


def _attn_kernel(pt_ref, lam_ref, sg_ref, q_ref, k_ref, v_ref, gate_ref,
                 qs_ref, kn_ref, vn_ref, gate_s_ref, *rest,
                 g_pages, n_steps, nq, tq, tk, unroll, t_len, lam_init):
    del pt_ref
    k_pages = rest[:g_pages]
    v_pages = rest[g_pages:2 * g_pages]
    o_ref, os_ref = rest[2 * g_pages:2 * g_pages + 2]
    p_scratch = rest[2 * g_pages + 2:2 * g_pages + 6]
    qbd_ref, m_ref, l_ref, acc_ref, kbuf, vbuf = rest[2 * g_pages + 6:]
    step = (pl.program_id(0) * N_HEADS + pl.program_id(1)) * nq + pl.program_id(2)
    j = step % n_steps

    @pl.when(j == 0)
    def _():
        _sample_init(qs_ref, qbd_ref, m_ref, l_ref, acc_ref, t_len=t_len)

    _attn_prompt_kernel(
        lam_ref, sg_ref, q_ref, k_ref, v_ref, gate_ref, o_ref, *p_scratch,
        tq=tq, tk=tk, unroll=unroll, lam_init=lam_init,
        before_diagonal=functools.partial(_sample_pages, k_pages, v_pages, qbd_ref, m_ref, l_ref,
                                          acc_ref, kbuf, vbuf))

    @pl.when(j == n_steps - 1)
    def _():
        _sample_finish(lam_ref, sg_ref, kn_ref, vn_ref, gate_s_ref, os_ref, qbd_ref, m_ref, l_ref,
                       acc_ref, t_len=t_len, lam_init=lam_init)


def _attention(page_table, lam_rows, subln_g, q_p, k_p, v_p, gate_p, n_seq_p, seq_len,
               q_s, k_new, v_new, gate_s, cache_k, cache_v, layer, n_seq_s, t_len, lam_init):
    assert t_len == SUBLANES and V_DIM == LANES
    tq, tk, unroll = 512, 512, 2
    assert tk % tq == 0 and seq_len % tk == 0 and unroll & (unroll - 1) == 0
    nq = seq_len // tq
    grid = (n_seq_p, N_HEADS, nq)
    n_grid = n_seq_p * N_HEADS * nq
    n_pages = page_table.shape[1]
    assert (n_seq_s * n_pages) % n_grid == 0
    g_pages = n_seq_s * n_pages // n_grid
    assert n_pages % g_pages == 0
    n_steps = n_pages // g_pages
    n_rows = N_HEADS * 2 * t_len
    pt_flat = page_table.reshape(-1)
    n_phys = cache_k.shape[1]
    page_rows = PAGE_SIZE * N_HEADS
    ck = cache_k.reshape(-1, V_DIM)
    cv = cache_v.reshape(-1, V_DIM)

    def sample_pos(b, h, i):
        step = (b * N_HEADS + h) * nq + i
        return step // n_steps, step % n_steps

    def page_spec(g):
        def index(b, h, i, pt):
            seq, j = sample_pos(b, h, i)
            return layer * n_phys + pt[seq * n_pages + j * g_pages + g], 0
        return pl.BlockSpec((page_rows, V_DIM), index)

    q_spec = pl.BlockSpec((tq, V_DIM), lambda b, h, i, pt: (b * nq + i, h))
    kv_spec = pl.BlockSpec((seq_len, V_DIM), lambda b, h, i, pt: (b, h))
    tok_spec = pl.BlockSpec((t_len, D_ATTN), lambda b, h, i, pt: (sample_pos(b, h, i)[0], 0))
    new_spec = pl.BlockSpec((t_len * N_HEADS, V_DIM),
                            lambda b, h, i, pt: (sample_pos(b, h, i)[0], 0))
    full = lambda shape: pl.BlockSpec(shape, lambda b, h, i, pt: (0,) * len(shape))
    grid_spec = pltpu.PrefetchScalarGridSpec(
        num_scalar_prefetch=1,
        grid=grid,
        in_specs=[full((4, HEAD_DIM)), full((1, V_DIM)), q_spec, kv_spec, kv_spec, q_spec,
                  tok_spec, new_spec, new_spec, tok_spec]
                 + [page_spec(g) for g in range(g_pages)]
                 + [page_spec(g) for g in range(g_pages)],
        out_specs=(q_spec, tok_spec),
        scratch_shapes=[pltpu.VMEM((2 * tq, V_DIM), BF16),
                        pltpu.VMEM((2 * tq, LANES), F32),
                        pltpu.VMEM((2 * tq, LANES), F32),
                        pltpu.VMEM((2 * tq, V_DIM), F32),
                        pltpu.VMEM((n_rows, D_ATTN), BF16),
                        pltpu.VMEM((n_rows, LANES), F32),
                        pltpu.VMEM((n_rows, LANES), F32),
                        pltpu.VMEM((n_rows, D_ATTN), F32),
                        pltpu.VMEM((g_pages * PAGE_SIZE, D_ATTN), BF16),
                        pltpu.VMEM((g_pages * PAGE_SIZE, D_ATTN), BF16)],
    )
    return pl.pallas_call(
        functools.partial(_attn_kernel, g_pages=g_pages, n_steps=n_steps, nq=nq, tq=tq, tk=tk,
                          unroll=unroll, t_len=t_len, lam_init=lam_init),
        out_shape=(jax.ShapeDtypeStruct((q_p.shape[0], D_ATTN), BF16),
                   jax.ShapeDtypeStruct((n_seq_s * t_len, D_ATTN), F32)),
        grid_spec=grid_spec,
        compiler_params=pltpu.CompilerParams(
            dimension_semantics=("arbitrary", "arbitrary", "arbitrary"),
            vmem_limit_bytes=VMEM_LIMIT),
        name="attention",
    )(pt_flat, lam_rows, subln_g, q_p, k_p, v_p, gate_p, q_s, k_new, v_new, gate_s,
      *([ck] * g_pages), *([cv] * g_pages))


def _outproj_kernel(x_ref, mc_ref, ma_ref, w_ref, g_ref, y_ref, *, final_norm):
    acc = jnp.dot(mc_ref[...].astype(BF16), w_ref[0:D_CONV, :], preferred_element_type=F32)
    acc = acc + jnp.dot(ma_ref[...].astype(BF16), w_ref[D_CONV:D_CONV + D_ATTN, :],
                        preferred_element_type=F32)
    y = x_ref[...] + acc
    y_ref[...] = _rmsnorm(y, g_ref[...]) if final_norm else y


def _outproj(x2d, mixc, mixa, w_bf, ln_f_g, final_norm):
    n_tok = x2d.shape[0]
    tm = 1024
    tok_spec = lambda width: pl.BlockSpec((tm, width), lambda t: (t, 0))
    full = lambda shape: pl.BlockSpec(shape, lambda t: (0,) * len(shape))
    return pl.pallas_call(
        functools.partial(_outproj_kernel, final_norm=final_norm),
        out_shape=jax.ShapeDtypeStruct((n_tok, D_MODEL), F32),
        grid=(n_tok // tm,),
        in_specs=[tok_spec(D_MODEL), tok_spec(D_CONV), tok_spec(D_ATTN),
                  full((D_CONV + D_ATTN, D_MODEL)), full((1, D_MODEL))],
        out_specs=tok_spec(D_MODEL),
        compiler_params=pltpu.CompilerParams(dimension_semantics=("arbitrary",),
                                             vmem_limit_bytes=VMEM_LIMIT),
        name="outproj",
    )(x2d, mixc, mixa, w_bf, ln_f_g)


def kernel(x_prompt, x_sample, cache_k, cache_v, state_conv, page_table, ln_in_g, w_in, conv_w,
           lambda_q1, lambda_k1, lambda_q2, lambda_k2, subln_g, w_out, ln_f_g):
    depth = w_in.shape[0]
    n_p, s_p, _ = x_prompt.shape
    n_s, t_s, _ = x_sample.shape
    xp = x_prompt.reshape(n_p * s_p, D_MODEL)
    xs = x_sample.reshape(n_s * t_s, D_MODEL)
    ln_f = ln_f_g.reshape(1, D_MODEL)
    kp_l, vp_l, cp_l, ks_l, vs_l, cs_l = [], [], [], [], [], []
    for layer in range(depth):
        lam_init = _lambda_init(layer)
        last = layer == depth - 1
        ln_g = ln_in_g[layer].reshape(1, D_MODEL)
        w_in_bf = w_in[layer].astype(BF16)
        w_out_bf = w_out[layer].astype(BF16)
        sg = subln_g[layer].reshape(1, V_DIM)
        lam_rows = jnp.stack([lambda_q1[layer], lambda_k1[layer],
                              lambda_q2[layer], lambda_k2[layer]])

        q_p, k_p, v_p, kb_p, vb_p, mixc_p, gate_p, conv_p = _inproj_prompt(
            xp, ln_g, w_in_bf, conv_w[layer], n_p, s_p)
        q_s, k_s, v_s, mixc_s, gate_s, conv_s = _inproj_sample(
            xs, ln_g, w_in_bf, conv_w[layer], state_conv[layer], n_s, t_s)
        mixa_p, mixa_s = _attention(
            page_table, lam_rows, sg, q_p, kb_p, vb_p, gate_p, n_p, s_p,
            q_s, k_s, v_s, gate_s, cache_k, cache_v, layer, n_s, t_s, lam_init)
        xp = _outproj(xp, mixc_p, mixa_p, w_out_bf, ln_f, last)
        xs = _outproj(xs, mixc_s, mixa_s, w_out_bf, ln_f, last)
        kp_l.append(k_p.reshape(n_p, s_p // PAGE_SIZE, PAGE_SIZE, N_HEADS, 2 * HEAD_DIM))
        vp_l.append(v_p.reshape(n_p, s_p // PAGE_SIZE, PAGE_SIZE, N_HEADS, V_DIM))
        cp_l.append(conv_p)
        ks_l.append(k_s.reshape(n_s, t_s, N_HEADS, 2 * HEAD_DIM))
        vs_l.append(v_s.reshape(n_s, t_s, N_HEADS, V_DIM))
        cs_l.append(conv_s)

    y_prompt = xp.reshape(n_p, s_p, D_MODEL)
    y_sample = xs.reshape(n_s, t_s, D_MODEL)
    return (y_prompt, y_sample, jnp.stack(kp_l), jnp.stack(vp_l), jnp.stack(cp_l),
            jnp.stack(ks_l), jnp.stack(vs_l), jnp.stack(cs_l))
```

```python
import functools
import math

import jax
import jax.numpy as jnp
from jax import lax
from jax.experimental import pallas as pl
from jax.experimental.pallas import tpu as pltpu

D_MODEL = 1024
D_CONV = 512
D_ATTN = 512
N_HEADS = 4
HEAD_DIM = 64
V_DIM = 2 * HEAD_DIM
CONV_W = 3
PAGE_SIZE = 128
PAGE_ROWS = PAGE_SIZE * N_HEADS
EPS = 1e-6
GROUP_W = 512
N_GROUPS = 8
SUBLANES = 8
LANES = 128
NEG_BIG = -1e30
VMEM_LIMIT = 56 * 1024 * 1024

F32 = jnp.float32
BF16 = jnp.bfloat16


def _lambda_init(layer):
    return 0.8 - 0.6 * math.exp(-0.3 * layer)


def _silu(z):
    return z / (1.0 + jnp.exp(-z))


def _rmsnorm(x, g):
    ms = jnp.mean(x * x, axis=-1, keepdims=True)
    return x * lax.rsqrt(ms + EPS) * g


def _proj(hb, w_ref, group):
    w = w_ref[:, group * GROUP_W:(group + 1) * GROUP_W]
    return jnp.dot(hb, w, preferred_element_type=F32)


def _diff_lambda(lam_ref, lam_init):
    lv = lam_ref[...]
    a = jnp.sum(lv[0:1] * lv[1:2], axis=-1, keepdims=True)
    b = jnp.sum(lv[2:3] * lv[3:4], axis=-1, keepdims=True)
    return jnp.exp(a) - jnp.exp(b) + lam_init


def _lane_tile(x, n):
    return x if n == 1 else jnp.concatenate([x] * n, axis=1)


def _online_softmax(s, v_bf, m, l, acc):
    m_new = jnp.maximum(m, jnp.max(s, axis=-1, keepdims=True))
    alpha = jnp.exp(m - m_new)
    p = jnp.exp(s - _lane_tile(m_new, s.shape[1] // LANES))
    l = alpha * l + jnp.sum(p, axis=-1, keepdims=True)
    pv = jnp.dot(p.astype(BF16), v_bf, preferred_element_type=F32)
    acc = _lane_tile(alpha, acc.shape[1] // LANES) * acc + pv
    return m_new, l, acc


def _online_softmax_step(s, v_bf, m_ref, l_ref, acc_ref):
    m_ref[...], l_ref[...], acc_ref[...] = _online_softmax(
        s, v_bf, m_ref[...], l_ref[...], acc_ref[...])


def _store_head_rows(ref, x, n_tok):
    for h in range(N_HEADS):
        ref[pl.ds(h, n_tok, stride=N_HEADS), :] = x[:, h * V_DIM:(h + 1) * V_DIM]


def _load_head_rows(ref, n_tok):
    return jnp.concatenate(
        [ref[pl.ds(h, n_tok, stride=N_HEADS), :] for h in range(N_HEADS)], axis=1)


def _subln_gate(o, sg, gate, lam_init):
    ms = jnp.mean(o * o, axis=-1, keepdims=True)
    return gate * (o * lax.rsqrt(ms + EPS) * sg * (1.0 - lam_init))


def _inproj_prompt_kernel(x_ref, g_ref, w_ref, cw_ref,
                          q_ref, k_ref, v_ref, kb_ref, vb_ref, mixc_ref, gate_ref, conv_ref,
                          ubuf, *, tiles_per_seq, tm):
    t = pl.program_id(0)
    hb = _rmsnorm(x_ref[...], g_ref[...]).astype(BF16)

    @pl.when(t % tiles_per_seq == 0)
    def _():
        ubuf[0:SUBLANES, :] = jnp.zeros((SUBLANES, D_CONV), F32)

    u = _proj(hb, w_ref, 1) * _proj(hb, w_ref, 2)
    ubuf[SUBLANES:SUBLANES + tm, :] = u
    cw = cw_ref[...]
    y = (cw[0:1] * ubuf[SUBLANES - 2:SUBLANES - 2 + tm, :]
         + cw[1:2] * ubuf[SUBLANES - 1:SUBLANES - 1 + tm, :]
         + cw[2:3] * u)
    yc = _proj(hb, w_ref, 0) * y
    mixc_ref[...] = (_silu(_proj(hb, w_ref, 3)) * yc).astype(BF16)
    ubuf[0:SUBLANES, :] = u[tm - SUBLANES:tm, :]

    @pl.when(t % tiles_per_seq == tiles_per_seq - 1)
    def _():
        conv_ref[0] = u[tm - (CONV_W - 1):tm, :]

    q_ref[...] = (_proj(hb, w_ref, 4) * (HEAD_DIM ** -0.5)).astype(BF16)
    k = _proj(hb, w_ref, 5)
    _store_head_rows(k_ref, k, tm)
    kb_ref[...] = k.astype(BF16)
    v = _proj(hb, w_ref, 6)
    _store_head_rows(v_ref, v, tm)
    vb_ref[...] = v.astype(BF16)
    gate_ref[...] = _silu(_proj(hb, w_ref, 7)).astype(BF16)


def _inproj_prompt(x2d, ln_g, w_bf, conv_w, n_seq, seq_len):
    n_tok = x2d.shape[0]
    tm = 1024
    tiles_per_seq = seq_len // tm
    n_tiles = n_tok // tm
    tok_spec = lambda width: pl.BlockSpec((tm, width), lambda t: (t, 0))
    head_rows = pl.BlockSpec((tm * N_HEADS, V_DIM), lambda t: (t, 0))
    full = lambda shape: pl.BlockSpec(shape, lambda t: (0,) * len(shape))
    out_shape = (
        jax.ShapeDtypeStruct((n_tok, D_ATTN), BF16),
        jax.ShapeDtypeStruct((n_tok * N_HEADS, V_DIM), F32),
        jax.ShapeDtypeStruct((n_tok * N_HEADS, V_DIM), F32),
        jax.ShapeDtypeStruct((n_tok, D_ATTN), BF16),
        jax.ShapeDtypeStruct((n_tok, D_ATTN), BF16),
        jax.ShapeDtypeStruct((n_tok, D_CONV), BF16),
        jax.ShapeDtypeStruct((n_tok, D_ATTN), BF16),
        jax.ShapeDtypeStruct((n_seq, CONV_W - 1, D_CONV), F32),
    )
    return pl.pallas_call(
        functools.partial(_inproj_prompt_kernel, tiles_per_seq=tiles_per_seq, tm=tm),
        out_shape=out_shape,
        grid=(n_tiles,),
        in_specs=[tok_spec(D_MODEL), full((1, D_MODEL)), full((D_MODEL, N_GROUPS * GROUP_W)),
                  full((CONV_W, D_CONV))],
        out_specs=(tok_spec(D_ATTN), head_rows, head_rows) + (tok_spec(D_ATTN),) * 4 + (
            pl.BlockSpec((1, CONV_W - 1, D_CONV), lambda t: (t // tiles_per_seq, 0, 0)),),
        scratch_shapes=[pltpu.VMEM((tm + SUBLANES, D_CONV), F32)],
        compiler_params=pltpu.CompilerParams(dimension_semantics=("arbitrary",),
                                             vmem_limit_bytes=VMEM_LIMIT),
        name="inproj_prompt",
    )(x2d, ln_g, w_bf, conv_w)


def _inproj_sample_kernel(x_ref, g_ref, w_ref, cw_ref, buf_ref,
                          q_ref, k_ref, v_ref, mixc_ref, gate_ref, conv_ref, *, n_seq, t_len):
    hb = _rmsnorm(x_ref[...], g_ref[...]).astype(BF16)
    n_tok = n_seq * t_len
    u3 = (_proj(hb, w_ref, 1) * _proj(hb, w_ref, 2)).reshape(n_seq, t_len, D_CONV)
    tpos = lax.broadcasted_iota(jnp.int32, (n_seq, t_len, D_CONV), 1)
    b0 = buf_ref[:, 0:1, :]
    b1 = buf_ref[:, 1:2, :]
    prev1 = jnp.where(tpos >= 1, pltpu.roll(u3, 1, 1), b1)
    prev2 = jnp.where(tpos >= 2, pltpu.roll(u3, 2, 1), jnp.where(tpos == 1, b1, b0))
    cw = cw_ref[...]
    y = cw[0:1][None] * prev2 + cw[1:2][None] * prev1 + cw[2:3][None] * u3
    yc = _proj(hb, w_ref, 0) * y.reshape(n_tok, D_CONV)
    mixc_ref[...] = _silu(_proj(hb, w_ref, 3)) * yc
    conv_ref[...] = u3[:, t_len - (CONV_W - 1):t_len, :]
    q_ref[...] = _proj(hb, w_ref, 4) * (HEAD_DIM ** -0.5)
    _store_head_rows(k_ref, _proj(hb, w_ref, 5), n_tok)
    _store_head_rows(v_ref, _proj(hb, w_ref, 6), n_tok)
    gate_ref[...] = _silu(_proj(hb, w_ref, 7))


def _inproj_sample(x2d, ln_g, w_bf, conv_w, buf, n_seq, t_len):
    assert t_len == SUBLANES
    n_tok = x2d.shape[0]
    tok = jax.ShapeDtypeStruct((n_tok, D_ATTN), F32)
    head_rows = jax.ShapeDtypeStruct((n_tok * N_HEADS, V_DIM), F32)
    return pl.pallas_call(
        functools.partial(_inproj_sample_kernel, n_seq=n_seq, t_len=t_len),
        out_shape=(tok, head_rows, head_rows, tok, tok,
                   jax.ShapeDtypeStruct((n_seq, CONV_W - 1, D_CONV), F32)),
        compiler_params=pltpu.CompilerParams(vmem_limit_bytes=VMEM_LIMIT),
        name="inproj_sample",
    )(x2d, ln_g, w_bf, conv_w, buf)


def _attn_prompt_kernel(lam_ref, sg_ref, q_ref, k_ref, v_ref, gate_ref, o_ref,
                        qs_ref, m_ref, l_ref, acc_ref, *, tq, tk, unroll, lam_init,
                        before_diagonal=None):
    qi = pl.program_id(2)
    q_start = qi * tq
    q = q_ref[...]
    lane = lax.broadcasted_iota(jnp.int32, (tq, V_DIM), 1)
    zero = jnp.zeros_like(q)
    qs_ref[0:tq, :] = jnp.where(lane < HEAD_DIM, q, zero)
    qs_ref[tq:2 * tq, :] = jnp.where(lane >= HEAD_DIM, q, zero)
    m_ref[...] = jnp.full(m_ref.shape, NEG_BIG, F32)
    l_ref[...] = jnp.zeros(l_ref.shape, F32)
    acc_ref[...] = jnp.zeros(acc_ref.shape, F32)

    def scores(ki):
        k_off = pl.multiple_of(ki * tk, tk)
        kblk = k_ref[pl.ds(k_off, tk), :]
        vblk = v_ref[pl.ds(k_off, tk), :]
        s = lax.dot_general(qs_ref[...], kblk, (((1,), (1,)), ((), ())),
                            preferred_element_type=F32)
        return s, vblk

    def full_tiles(first, count):
        state = (m_ref[...], l_ref[...], acc_ref[...])
        for u in range(count):
            s, vblk = scores(first + u)
            state = _online_softmax(s, vblk, *state)
        m_ref[...], l_ref[...], acc_ref[...] = state

    def group(gi, carry):
        full_tiles(gi * unroll, unroll)
        return carry

    n_full = q_start // tk
    n_groups = n_full // unroll
    lax.fori_loop(0, n_groups, group, 0)
    done = n_groups * unroll
    part = unroll // 2
    while part >= 1:
        take = (n_full - done) & part

        @pl.when(take != 0)
        def _(done=done, part=part):
            full_tiles(done, part)

        done = done + take
        part //= 2

    if before_diagonal is not None:
        before_diagonal()

    s, vblk = scores(n_full)
    row = lax.broadcasted_iota(jnp.int32, (2 * tq, tk), 0)
    col = lax.broadcasted_iota(jnp.int32, (2 * tq, tk), 1)
    q_pos = q_start + jnp.where(row >= tq, row - tq, row)
    s = jnp.where(n_full * tk + col <= q_pos, s, NEG_BIG)
    _, l, acc = _online_softmax(s, vblk, m_ref[...], l_ref[...], acc_ref[...])

    o = acc[0:tq] / l[0:tq] - _diff_lambda(lam_ref, lam_init) * (acc[tq:2 * tq] / l[tq:2 * tq])
    o_ref[...] = _subln_gate(o, sg_ref[...], gate_ref[...].astype(F32), lam_init).astype(BF16)


def _sample_init(q_ref, qbd_ref, m_ref, l_ref, acc_ref, *, t_len):
    n_rows = N_HEADS * 2 * t_len
    qt = jnp.concatenate([q_ref[...]] * (N_HEADS * 2), axis=0)
    row = lax.broadcasted_iota(jnp.int32, (n_rows, D_ATTN), 0)
    col = lax.broadcasted_iota(jnp.int32, (n_rows, D_ATTN), 1)
    keep = (col // HEAD_DIM) == (row // t_len)
    qbd_ref[...] = jnp.where(keep, qt, 0.0).astype(BF16)
    m_ref[...] = jnp.full(m_ref.shape, NEG_BIG, F32)
    l_ref[...] = jnp.zeros(l_ref.shape, F32)
    acc_ref[...] = jnp.zeros(acc_ref.shape, F32)


def _sample_pages(k_refs, v_refs, qbd_ref, m_ref, l_ref, acc_ref, kbuf, vbuf):
    for g in range(len(k_refs)):
        rows = slice(g * PAGE_SIZE, (g + 1) * PAGE_SIZE)
        kbuf[rows, :] = _load_head_rows(k_refs[g], PAGE_SIZE).astype(BF16)
        vbuf[rows, :] = _load_head_rows(v_refs[g], PAGE_SIZE).astype(BF16)
    s = lax.dot_general(qbd_ref[...], kbuf[...], (((1,), (1,)), ((), ())),
                        preferred_element_type=F32)
    _online_softmax_step(s, vbuf[...], m_ref, l_ref, acc_ref)


def _sample_finish(lam_ref, sg_ref, kn_ref, vn_ref, gate_ref, o_ref, qbd_ref, m_ref, l_ref,
                   acc_ref, *, t_len, lam_init):
    n_rows = N_HEADS * 2 * t_len
    pad = jnp.zeros((LANES - t_len, D_ATTN), F32)
    kn = jnp.concatenate([_load_head_rows(kn_ref, t_len), pad], axis=0).astype(BF16)
    vn = jnp.concatenate([_load_head_rows(vn_ref, t_len), pad], axis=0).astype(BF16)
    sn = lax.dot_general(qbd_ref[...], kn, (((1,), (1,)), ((), ())),
                         preferred_element_type=F32)
    row = lax.broadcasted_iota(jnp.int32, (n_rows, LANES), 0)
    col = lax.broadcasted_iota(jnp.int32, (n_rows, LANES), 1)
    sn = jnp.where(col <= row % t_len, sn, NEG_BIG)
    _, l, acc = _online_softmax(sn, vn, m_ref[...], l_ref[...], acc_ref[...])

    lam = _diff_lambda(lam_ref, lam_init)
    sg = sg_ref[...]
    gate = gate_ref[...]
    for h in range(N_HEADS):
        r1 = (2 * h) * t_len
        r2 = (2 * h + 1) * t_len
        lanes = slice(h * V_DIM, (h + 1) * V_DIM)
        o = (acc[r1:r1 + t_len, lanes] / l[r1:r1 + t_len]
             - lam * (acc[r2:r2 + t_len, lanes] / l[r2:r2 + t_len]))
        o_ref[:, lanes] = _subln_gate(o, sg, gate[:, lanes], lam_init)


def _page_copies(pt_ref, cache_hbm, slot_ref, sem, step, *, g_pages, n_steps, n_pages, row0):
    seq = step // n_steps
    first = seq * n_pages + (step % n_steps) * g_pages
    copies = []
    for g in range(g_pages):
        rows = pl.multiple_of(row0 + pt_ref[first + g] * PAGE_ROWS, PAGE_ROWS)
        copies.append(pltpu.make_async_copy(
            cache_hbm.at[pl.ds(rows, PAGE_ROWS), :], slot_ref.at[g], sem))
    return copies


def _attn_kernel(pt_ref, lam_ref, sg_ref, q_ref, k_ref, v_ref, gate_ref,
                 qs_ref, kn_ref, vn_ref, gate_s_ref, ck_hbm, cv_hbm, o_ref, os_ref, *scratch,
                 g_pages, n_steps, n_pages, n_grid, row0, nq, tq, tk, unroll, t_len, lam_init):
    p_scratch = scratch[:4]
    qbd_ref, m_ref, l_ref, acc_ref, kbuf, vbuf, kpg, vpg, sem = scratch[4:]
    step = (pl.program_id(0) * N_HEADS + pl.program_id(1)) * nq + pl.program_id(2)
    j = step % n_steps
    slot = step % 2

    def copies(of_step, of_slot):
        kw = dict(g_pages=g_pages, n_steps=n_steps, n_pages=n_pages, row0=row0)
        return (_page_copies(pt_ref, ck_hbm, kpg.at[of_slot], sem.at[of_slot, 0], of_step, **kw)
                + _page_copies(pt_ref, cv_hbm, vpg.at[of_slot], sem.at[of_slot, 1], of_step, **kw))

    @pl.when(step == 0)
    def _():
        for c in copies(step, slot):
            c.start()

    @pl.when(step + 1 < n_grid)
    def _():
        for c in copies(step + 1, 1 - slot):
            c.start()

    @pl.when(j == 0)
    def _():
        _sample_init(qs_ref, qbd_ref, m_ref, l_ref, acc_ref, t_len=t_len)

    def sample_pages():
        for c in copies(step, slot):
            c.wait()
        k_slot, v_slot = kpg.at[slot], vpg.at[slot]
        _sample_pages([k_slot.at[g] for g in range(g_pages)],
                      [v_slot.at[g] for g in range(g_pages)],
                      qbd_ref, m_ref, l_ref, acc_ref, kbuf, vbuf)

    _attn_prompt_kernel(
        lam_ref, sg_ref, q_ref, k_ref, v_ref, gate_ref, o_ref, *p_scratch,
        tq=tq, tk=tk, unroll=unroll, lam_init=lam_init, before_diagonal=sample_pages)

    @pl.when(j == n_steps - 1)
    def _():
        _sample_finish(lam_ref, sg_ref, kn_ref, vn_ref, gate_s_ref, os_ref, qbd_ref, m_ref, l_ref,
                       acc_ref, t_len=t_len, lam_init=lam_init)


def _attention(page_table, lam_rows, subln_g, q_p, k_p, v_p, gate_p, n_seq_p, seq_len,
               q_s, k_new, v_new, gate_s, cache_k, cache_v, layer, n_seq_s, t_len, lam_init):
    assert t_len == SUBLANES and V_DIM == LANES
    tq, tk, unroll = 512, 512, 2
    assert tk % tq == 0 and seq_len % tk == 0 and unroll & (unroll - 1) == 0
    nq = seq_len // tq
    grid = (n_seq_p, N_HEADS, nq)
    n_grid = n_seq_p * N_HEADS * nq
    n_pages = page_table.shape[1]
    assert (n_seq_s * n_pages) % n_grid == 0
    g_pages = n_seq_s * n_pages // n_grid
    assert n_pages % g_pages == 0
    n_steps = n_pages // g_pages
    n_rows = N_HEADS * 2 * t_len
    pt_flat = page_table.reshape(-1)
    n_phys = cache_k.shape[1]
    ck = cache_k.reshape(-1, V_DIM)
    cv = cache_v.reshape(-1, V_DIM)

    def sample_pos(b, h, i):
        step = (b * N_HEADS + h) * nq + i
        return step // n_steps, step % n_steps

    q_spec = pl.BlockSpec((tq, V_DIM), lambda b, h, i, pt: (b * nq + i, h))
    kv_spec = pl.BlockSpec((seq_len, V_DIM), lambda b, h, i, pt: (b, h))
    tok_spec = pl.BlockSpec((t_len, D_ATTN), lambda b, h, i, pt: (sample_pos(b, h, i)[0], 0))
    new_spec = pl.BlockSpec((t_len * N_HEADS, V_DIM),
                            lambda b, h, i, pt: (sample_pos(b, h, i)[0], 0))
    full = lambda shape: pl.BlockSpec(shape, lambda b, h, i, pt: (0,) * len(shape))
    grid_spec = pltpu.PrefetchScalarGridSpec(
        num_scalar_prefetch=1,
        grid=grid,
        in_specs=[full((4, HEAD_DIM)), full((1, V_DIM)), q_spec, kv_spec, kv_spec, q_spec,
                  tok_spec, new_spec, new_spec, tok_spec,
                  pl.BlockSpec(memory_space=pl.ANY), pl.BlockSpec(memory_space=pl.ANY)],
        out_specs=(q_spec, tok_spec),
        scratch_shapes=[pltpu.VMEM((2 * tq, V_DIM), BF16),
                        pltpu.VMEM((2 * tq, LANES), F32),
                        pltpu.VMEM((2 * tq, LANES), F32),
                        pltpu.VMEM((2 * tq, V_DIM), F32),
                        pltpu.VMEM((n_rows, D_ATTN), BF16),
                        pltpu.VMEM((n_rows, LANES), F32),
                        pltpu.VMEM((n_rows, LANES), F32),
                        pltpu.VMEM((n_rows, D_ATTN), F32),
                        pltpu.VMEM((g_pages * PAGE_SIZE, D_ATTN), BF16),
                        pltpu.VMEM((g_pages * PAGE_SIZE, D_ATTN), BF16),
                        pltpu.VMEM((2, g_pages, PAGE_ROWS, V_DIM), F32),
                        pltpu.VMEM((2, g_pages, PAGE_ROWS, V_DIM), F32),
                        pltpu.SemaphoreType.DMA((2, 2))],
    )
    return pl.pallas_call(
        functools.partial(_attn_kernel, g_pages=g_pages, n_steps=n_steps, n_pages=n_pages,
                          n_grid=n_grid, row0=layer * n_phys * PAGE_ROWS, nq=nq, tq=tq, tk=tk,
                          unroll=unroll, t_len=t_len, lam_init=lam_init),
        out_shape=(jax.ShapeDtypeStruct((q_p.shape[0], D_ATTN), BF16),
                   jax.ShapeDtypeStruct((n_seq_s * t_len, D_ATTN), F32)),
        grid_spec=grid_spec,
        compiler_params=pltpu.CompilerParams(
            dimension_semantics=("arbitrary", "arbitrary", "arbitrary"),
            vmem_limit_bytes=VMEM_LIMIT),
        name="attention",
    )(pt_flat, lam_rows, subln_g, q_p, k_p, v_p, gate_p, q_s, k_new, v_new, gate_s, ck, cv)


def _outproj_kernel(x_ref, mc_ref, ma_ref, w_ref, g_ref, y_ref, *, final_norm):
    acc = jnp.dot(mc_ref[...].astype(BF16), w_ref[0:D_CONV, :], preferred_element_type=F32)
    acc = acc + jnp.dot(ma_ref[...].astype(BF16), w_ref[D_CONV:D_CONV + D_ATTN, :],
                        preferred_element_type=F32)
    y = x_ref[...] + acc
    y_ref[...] = _rmsnorm(y, g_ref[...]) if final_norm else y


def _outproj(x2d, mixc, mixa, w_bf, ln_f_g, final_norm):
    n_tok = x2d.shape[0]
    tm = 1024
    tok_spec = lambda width: pl.BlockSpec((tm, width), lambda t: (t, 0))
    full = lambda shape: pl.BlockSpec(shape, lambda t: (0,) * len(shape))
    return pl.pallas_call(
        functools.partial(_outproj_kernel, final_norm=final_norm),
        out_shape=jax.ShapeDtypeStruct((n_tok, D_MODEL), F32),
        grid=(n_tok // tm,),
        in_specs=[tok_spec(D_MODEL), tok_spec(D_CONV), tok_spec(D_ATTN),
                  full((D_CONV + D_ATTN, D_MODEL)), full((1, D_MODEL))],
        out_specs=tok_spec(D_MODEL),
        compiler_params=pltpu.CompilerParams(dimension_semantics=("arbitrary",),
                                             vmem_limit_bytes=VMEM_LIMIT),
        name="outproj",
    )(x2d, mixc, mixa, w_bf, ln_f_g)


def kernel(x_prompt, x_sample, cache_k, cache_v, state_conv, page_table, ln_in_g, w_in, conv_w,
           lambda_q1, lambda_k1, lambda_q2, lambda_k2, subln_g, w_out, ln_f_g):
    depth = w_in.shape[0]
    n_p, s_p, _ = x_prompt.shape
    n_s, t_s, _ = x_sample.shape
    xp = x_prompt.reshape(n_p * s_p, D_MODEL)
    xs = x_sample.reshape(n_s * t_s, D_MODEL)
    ln_f = ln_f_g.reshape(1, D_MODEL)
    kp_l, vp_l, cp_l, ks_l, vs_l, cs_l = [], [], [], [], [], []
    for layer in range(depth):
        lam_init = _lambda_init(layer)
        last = layer == depth - 1
        ln_g = ln_in_g[layer].reshape(1, D_MODEL)
        w_in_bf = w_in[layer].astype(BF16)
        w_out_bf = w_out[layer].astype(BF16)
        sg = subln_g[layer].reshape(1, V_DIM)
        lam_rows = jnp.stack([lambda_q1[layer], lambda_k1[layer],
                              lambda_q2[layer], lambda_k2[layer]])

        q_p, k_p, v_p, kb_p, vb_p, mixc_p, gate_p, conv_p = _inproj_prompt(
            xp, ln_g, w_in_bf, conv_w[layer], n_p, s_p)
        q_s, k_s, v_s, mixc_s, gate_s, conv_s = _inproj_sample(
            xs, ln_g, w_in_bf, conv_w[layer], state_conv[layer], n_s, t_s)
        mixa_p, mixa_s = _attention(
            page_table, lam_rows, sg, q_p, kb_p, vb_p, gate_p, n_p, s_p,
            q_s, k_s, v_s, gate_s, cache_k, cache_v, layer, n_s, t_s, lam_init)
        xp = _outproj(xp, mixc_p, mixa_p, w_out_bf, ln_f, last)
        xs = _outproj(xs, mixc_s, mixa_s, w_out_bf, ln_f, last)
        kp_l.append(k_p.reshape(n_p, s_p // PAGE_SIZE, PAGE_SIZE, N_HEADS, 2 * HEAD_DIM))
        vp_l.append(v_p.reshape(n_p, s_p // PAGE_SIZE, PAGE_SIZE, N_HEADS, V_DIM))
        cp_l.append(conv_p)
        ks_l.append(k_s.reshape(n_s, t_s, N_HEADS, 2 * HEAD_DIM))
        vs_l.append(v_s.reshape(n_s, t_s, N_HEADS, V_DIM))
        cs_l.append(conv_s)

    y_prompt = xp.reshape(n_p, s_p, D_MODEL)
    y_sample = xs.reshape(n_s, t_s, D_MODEL)
    return (y_prompt, y_sample, jnp.stack(kp_l), jnp.stack(vp_l), jnp.stack(cp_l),
            jnp.stack(ks_l), jnp.stack(vs_l), jnp.stack(cs_l))
```

```python
import functools
import math

import jax
import jax.numpy as jnp
from jax import lax
from jax.experimental import pallas as pl
from jax.experimental.pallas import tpu as pltpu

D_MODEL = 1024
D_CONV = 512
D_ATTN = 512
N_HEADS = 4
HEAD_DIM = 64
V_DIM = 2 * HEAD_DIM
CONV_W = 3
PAGE_SIZE = 128
PAGE_ROWS = PAGE_SIZE * N_HEADS
EPS = 1e-6
GROUP_W = 512
N_GROUPS = 8
SUBLANES = 8
LANES = 128
NEG_BIG = -1e30
LOG2_E = math.log2(math.e)
VMEM_LIMIT = 56 * 1024 * 1024

F32 = jnp.float32
BF16 = jnp.bfloat16


def _lambda_init(layer):
    return 0.8 - 0.6 * math.exp(-0.3 * layer)


def _silu(z):
    return z / (1.0 + jnp.exp(-z))


def _rmsnorm(x, g):
    ms = jnp.mean(x * x, axis=-1, keepdims=True)
    return x * lax.rsqrt(ms + EPS) * g


def _proj(hb, w_ref, group):
    w = w_ref[:, group * GROUP_W:(group + 1) * GROUP_W]
    return jnp.dot(hb, w, preferred_element_type=F32)


def _diff_lambda(lam_ref, lam_init):
    lv = lam_ref[...]
    a = jnp.sum(lv[0:1] * lv[1:2], axis=-1, keepdims=True)
    b = jnp.sum(lv[2:3] * lv[3:4], axis=-1, keepdims=True)
    return jnp.exp(a) - jnp.exp(b) + lam_init


def _lane_tile(x, n):
    return x if n == 1 else jnp.concatenate([x] * n, axis=1)


def _online_softmax(s, v_bf, m, l, acc, exp=jnp.exp):
    m_new = jnp.maximum(m, jnp.max(s, axis=-1, keepdims=True))
    alpha = exp(m - m_new)
    p = exp(s - _lane_tile(m_new, s.shape[1] // LANES))
    l = alpha * l + jnp.sum(p, axis=-1, keepdims=True)
    pv = jnp.dot(p.astype(BF16), v_bf, preferred_element_type=F32)
    acc = _lane_tile(alpha, acc.shape[1] // LANES) * acc + pv
    return m_new, l, acc


def _online_softmax_step(s, v_bf, m_ref, l_ref, acc_ref):
    m_ref[...], l_ref[...], acc_ref[...] = _online_softmax(
        s, v_bf, m_ref[...], l_ref[...], acc_ref[...])


def _store_head_rows(ref, x, n_tok):
    for h in range(N_HEADS):
        ref[pl.ds(h, n_tok, stride=N_HEADS), :] = x[:, h * V_DIM:(h + 1) * V_DIM]


def _load_head_rows(ref, n_tok):
    return jnp.concatenate(
        [ref[pl.ds(h, n_tok, stride=N_HEADS), :] for h in range(N_HEADS)], axis=1)


def _subln_gate(o, sg, gate, lam_init):
    ms = jnp.mean(o * o, axis=-1, keepdims=True)
    return gate * (o * lax.rsqrt(ms + EPS) * sg * (1.0 - lam_init))


def _inproj_prompt_kernel(x_ref, g_ref, w_ref, cw_ref,
                          q_ref, k_ref, v_ref, kb_ref, vb_ref, mixc_ref, gate_ref, conv_ref,
                          ubuf, *, tiles_per_seq, tm):
    t = pl.program_id(0)
    hb = _rmsnorm(x_ref[...], g_ref[...]).astype(BF16)

    @pl.when(t % tiles_per_seq == 0)
    def _():
        ubuf[0:SUBLANES, :] = jnp.zeros((SUBLANES, D_CONV), F32)

    u = _proj(hb, w_ref, 1) * _proj(hb, w_ref, 2)
    ubuf[SUBLANES:SUBLANES + tm, :] = u
    cw = cw_ref[...]
    y = (cw[0:1] * ubuf[SUBLANES - 2:SUBLANES - 2 + tm, :]
         + cw[1:2] * ubuf[SUBLANES - 1:SUBLANES - 1 + tm, :]
         + cw[2:3] * u)
    yc = _proj(hb, w_ref, 0) * y
    mixc_ref[...] = (_silu(_proj(hb, w_ref, 3)) * yc).astype(BF16)
    ubuf[0:SUBLANES, :] = u[tm - SUBLANES:tm, :]

    @pl.when(t % tiles_per_seq == tiles_per_seq - 1)
    def _():
        conv_ref[0] = u[tm - (CONV_W - 1):tm, :]

    q_ref[...] = (_proj(hb, w_ref, 4) * (HEAD_DIM ** -0.5 * LOG2_E)).astype(BF16)
    k = _proj(hb, w_ref, 5)
    _store_head_rows(k_ref, k, tm)
    kb_ref[...] = k.astype(BF16)
    v = _proj(hb, w_ref, 6)
    _store_head_rows(v_ref, v, tm)
    vb_ref[...] = v.astype(BF16)
    gate_ref[...] = _silu(_proj(hb, w_ref, 7)).astype(BF16)


def _inproj_prompt(x2d, ln_g, w_bf, conv_w, n_seq, seq_len):
    n_tok = x2d.shape[0]
    tm = 1024
    tiles_per_seq = seq_len // tm
    n_tiles = n_tok // tm
    tok_spec = lambda width: pl.BlockSpec((tm, width), lambda t: (t, 0))
    head_rows = pl.BlockSpec((tm * N_HEADS, V_DIM), lambda t: (t, 0))
    full = lambda shape: pl.BlockSpec(shape, lambda t: (0,) * len(shape))
    out_shape = (
        jax.ShapeDtypeStruct((n_tok, D_ATTN), BF16),
        jax.ShapeDtypeStruct((n_tok * N_HEADS, V_DIM), F32),
        jax.ShapeDtypeStruct((n_tok * N_HEADS, V_DIM), F32),
        jax.ShapeDtypeStruct((n_tok, D_ATTN), BF16),
        jax.ShapeDtypeStruct((n_tok, D_ATTN), BF16),
        jax.ShapeDtypeStruct((n_tok, D_CONV), BF16),
        jax.ShapeDtypeStruct((n_tok, D_ATTN), BF16),
        jax.ShapeDtypeStruct((n_seq, CONV_W - 1, D_CONV), F32),
    )
    return pl.pallas_call(
        functools.partial(_inproj_prompt_kernel, tiles_per_seq=tiles_per_seq, tm=tm),
        out_shape=out_shape,
        grid=(n_tiles,),
        in_specs=[tok_spec(D_MODEL), full((1, D_MODEL)), full((D_MODEL, N_GROUPS * GROUP_W)),
                  full((CONV_W, D_CONV))],
        out_specs=(tok_spec(D_ATTN), head_rows, head_rows) + (tok_spec(D_ATTN),) * 4 + (
            pl.BlockSpec((1, CONV_W - 1, D_CONV), lambda t: (t // tiles_per_seq, 0, 0)),),
        scratch_shapes=[pltpu.VMEM((tm + SUBLANES, D_CONV), F32)],
        compiler_params=pltpu.CompilerParams(dimension_semantics=("arbitrary",),
                                             vmem_limit_bytes=VMEM_LIMIT),
        name="inproj_prompt",
    )(x2d, ln_g, w_bf, conv_w)


def _inproj_sample_kernel(x_ref, g_ref, w_ref, cw_ref, buf_ref,
                          q_ref, k_ref, v_ref, mixc_ref, gate_ref, conv_ref, *, n_seq, t_len):
    hb = _rmsnorm(x_ref[...], g_ref[...]).astype(BF16)
    n_tok = n_seq * t_len
    u3 = (_proj(hb, w_ref, 1) * _proj(hb, w_ref, 2)).reshape(n_seq, t_len, D_CONV)
    tpos = lax.broadcasted_iota(jnp.int32, (n_seq, t_len, D_CONV), 1)
    b0 = buf_ref[:, 0:1, :]
    b1 = buf_ref[:, 1:2, :]
    prev1 = jnp.where(tpos >= 1, pltpu.roll(u3, 1, 1), b1)
    prev2 = jnp.where(tpos >= 2, pltpu.roll(u3, 2, 1), jnp.where(tpos == 1, b1, b0))
    cw = cw_ref[...]
    y = cw[0:1][None] * prev2 + cw[1:2][None] * prev1 + cw[2:3][None] * u3
    yc = _proj(hb, w_ref, 0) * y.reshape(n_tok, D_CONV)
    mixc_ref[...] = _silu(_proj(hb, w_ref, 3)) * yc
    conv_ref[...] = u3[:, t_len - (CONV_W - 1):t_len, :]
    q_ref[...] = _proj(hb, w_ref, 4) * (HEAD_DIM ** -0.5)
    _store_head_rows(k_ref, _proj(hb, w_ref, 5), n_tok)
    _store_head_rows(v_ref, _proj(hb, w_ref, 6), n_tok)
    gate_ref[...] = _silu(_proj(hb, w_ref, 7))


def _inproj_sample(x2d, ln_g, w_bf, conv_w, buf, n_seq, t_len):
    assert t_len == SUBLANES
    n_tok = x2d.shape[0]
    tok = jax.ShapeDtypeStruct((n_tok, D_ATTN), F32)
    head_rows = jax.ShapeDtypeStruct((n_tok * N_HEADS, V_DIM), F32)
    return pl.pallas_call(
        functools.partial(_inproj_sample_kernel, n_seq=n_seq, t_len=t_len),
        out_shape=(tok, head_rows, head_rows, tok, tok,
                   jax.ShapeDtypeStruct((n_seq, CONV_W - 1, D_CONV), F32)),
        compiler_params=pltpu.CompilerParams(vmem_limit_bytes=VMEM_LIMIT),
        name="inproj_sample",
    )(x2d, ln_g, w_bf, conv_w, buf)


def _attn_prompt_kernel(lam_ref, sg_ref, q_ref, k_ref, v_ref, gate_ref, o_ref,
                        qs_ref, m_ref, l_ref, acc_ref, *, tq, tk, unroll, lam_init,
                        before_diagonal=None):
    qi = pl.program_id(2)
    q_start = qi * tq
    q = q_ref[...]
    lane = lax.broadcasted_iota(jnp.int32, (tq, V_DIM), 1)
    zero = jnp.zeros_like(q)
    qs_ref[0:tq, :] = jnp.where(lane < HEAD_DIM, q, zero)
    qs_ref[tq:2 * tq, :] = jnp.where(lane >= HEAD_DIM, q, zero)
    m_ref[...] = jnp.full(m_ref.shape, NEG_BIG, F32)
    l_ref[...] = jnp.zeros(l_ref.shape, F32)
    acc_ref[...] = jnp.zeros(acc_ref.shape, F32)

    def scores(ki):
        k_off = pl.multiple_of(ki * tk, tk)
        kblk = k_ref[pl.ds(k_off, tk), :]
        vblk = v_ref[pl.ds(k_off, tk), :]
        s = lax.dot_general(qs_ref[...], kblk, (((1,), (1,)), ((), ())),
                            preferred_element_type=F32)
        return s, vblk

    def full_tiles(first, count):
        state = (m_ref[...], l_ref[...], acc_ref[...])
        for u in range(count):
            s, vblk = scores(first + u)
            state = _online_softmax(s, vblk, *state, exp=jnp.exp2)
        m_ref[...], l_ref[...], acc_ref[...] = state

    def group(gi, carry):
        full_tiles(gi * unroll, unroll)
        return carry

    n_full = q_start // tk
    n_groups = n_full // unroll
    lax.fori_loop(0, n_groups, group, 0)
    done = n_groups * unroll
    part = unroll // 2
    while part >= 1:
        take = (n_full - done) & part

        @pl.when(take != 0)
        def _(done=done, part=part):
            full_tiles(done, part)

        done = done + take
        part //= 2

    if before_diagonal is not None:
        before_diagonal()

    s, vblk = scores(n_full)
    row = lax.broadcasted_iota(jnp.int32, (2 * tq, tk), 0)
    col = lax.broadcasted_iota(jnp.int32, (2 * tq, tk), 1)
    q_pos = q_start + jnp.where(row >= tq, row - tq, row)
    s = jnp.where(n_full * tk + col <= q_pos, s, NEG_BIG)
    _, l, acc = _online_softmax(s, vblk, m_ref[...], l_ref[...], acc_ref[...], exp=jnp.exp2)

    o = acc[0:tq] / l[0:tq] - _diff_lambda(lam_ref, lam_init) * (acc[tq:2 * tq] / l[tq:2 * tq])
    o_ref[...] = _subln_gate(o, sg_ref[...], gate_ref[...].astype(F32), lam_init).astype(BF16)


def _sample_init(q_ref, qbd_ref, m_ref, l_ref, acc_ref, *, t_len):
    n_rows = N_HEADS * 2 * t_len
    qt = jnp.concatenate([q_ref[...]] * (N_HEADS * 2), axis=0)
    row = lax.broadcasted_iota(jnp.int32, (n_rows, D_ATTN), 0)
    col = lax.broadcasted_iota(jnp.int32, (n_rows, D_ATTN), 1)
    keep = (col // HEAD_DIM) == (row // t_len)
    qbd_ref[...] = jnp.where(keep, qt, 0.0).astype(BF16)
    m_ref[...] = jnp.full(m_ref.shape, NEG_BIG, F32)
    l_ref[...] = jnp.zeros(l_ref.shape, F32)
    acc_ref[...] = jnp.zeros(acc_ref.shape, F32)


def _sample_pages(k_refs, v_refs, qbd_ref, m_ref, l_ref, acc_ref, kbuf, vbuf):
    for g in range(len(k_refs)):
        rows = slice(g * PAGE_SIZE, (g + 1) * PAGE_SIZE)
        kbuf[rows, :] = _load_head_rows(k_refs[g], PAGE_SIZE).astype(BF16)
        vbuf[rows, :] = _load_head_rows(v_refs[g], PAGE_SIZE).astype(BF16)
    s = lax.dot_general(qbd_ref[...], kbuf[...], (((1,), (1,)), ((), ())),
                        preferred_element_type=F32)
    _online_softmax_step(s, vbuf[...], m_ref, l_ref, acc_ref)


def _sample_finish(lam_ref, sg_ref, kn_ref, vn_ref, gate_ref, o_ref, qbd_ref, m_ref, l_ref,
                   acc_ref, *, t_len, lam_init):
    n_rows = N_HEADS * 2 * t_len
    pad = jnp.zeros((LANES - t_len, D_ATTN), F32)
    kn = jnp.concatenate([_load_head_rows(kn_ref, t_len), pad], axis=0).astype(BF16)
    vn = jnp.concatenate([_load_head_rows(vn_ref, t_len), pad], axis=0).astype(BF16)
    sn = lax.dot_general(qbd_ref[...], kn, (((1,), (1,)), ((), ())),
                         preferred_element_type=F32)
    row = lax.broadcasted_iota(jnp.int32, (n_rows, LANES), 0)
    col = lax.broadcasted_iota(jnp.int32, (n_rows, LANES), 1)
    sn = jnp.where(col <= row % t_len, sn, NEG_BIG)
    _, l, acc = _online_softmax(sn, vn, m_ref[...], l_ref[...], acc_ref[...])

    lam = _diff_lambda(lam_ref, lam_init)
    sg = sg_ref[...]
    gate = gate_ref[...]
    for h in range(N_HEADS):
        r1 = (2 * h) * t_len
        r2 = (2 * h + 1) * t_len
        lanes = slice(h * V_DIM, (h + 1) * V_DIM)
        o = (acc[r1:r1 + t_len, lanes] / l[r1:r1 + t_len]
             - lam * (acc[r2:r2 + t_len, lanes] / l[r2:r2 + t_len]))
        o_ref[:, lanes] = _subln_gate(o, sg, gate[:, lanes], lam_init)


def _page_copies(pt_ref, cache_hbm, slot_ref, sem, step, *, g_pages, n_steps, n_pages, row0):
    seq = step // n_steps
    first = seq * n_pages + (step % n_steps) * g_pages
    copies = []
    for g in range(g_pages):
        rows = pl.multiple_of(row0 + pt_ref[first + g] * PAGE_ROWS, PAGE_ROWS)
        copies.append(pltpu.make_async_copy(
            cache_hbm.at[pl.ds(rows, PAGE_ROWS), :], slot_ref.at[g], sem))
    return copies


def _attn_kernel(pt_ref, lam_ref, sg_ref, q_ref, k_ref, v_ref, gate_ref,
                 qs_ref, kn_ref, vn_ref, gate_s_ref, ck_hbm, cv_hbm, o_ref, os_ref, *scratch,
                 g_pages, n_steps, n_pages, n_grid, row0, nq, tq, tk, unroll, t_len, lam_init):
    p_scratch = scratch[:4]
    qbd_ref, m_ref, l_ref, acc_ref, kbuf, vbuf, kpg, vpg, sem = scratch[4:]
    step = (pl.program_id(0) * N_HEADS + pl.program_id(1)) * nq + pl.program_id(2)
    j = step % n_steps
    slot = step % 2

    def copies(of_step, of_slot):
        kw = dict(g_pages=g_pages, n_steps=n_steps, n_pages=n_pages, row0=row0)
        return (_page_copies(pt_ref, ck_hbm, kpg.at[of_slot], sem.at[of_slot, 0], of_step, **kw)
                + _page_copies(pt_ref, cv_hbm, vpg.at[of_slot], sem.at[of_slot, 1], of_step, **kw))

    @pl.when(step == 0)
    def _():
        for c in copies(step, slot):
            c.start()

    @pl.when(step + 1 < n_grid)
    def _():
        for c in copies(step + 1, 1 - slot):
            c.start()

    @pl.when(j == 0)
    def _():
        _sample_init(qs_ref, qbd_ref, m_ref, l_ref, acc_ref, t_len=t_len)

    def sample_pages():
        for c in copies(step, slot):
            c.wait()
        k_slot, v_slot = kpg.at[slot], vpg.at[slot]
        _sample_pages([k_slot.at[g] for g in range(g_pages)],
                      [v_slot.at[g] for g in range(g_pages)],
                      qbd_ref, m_ref, l_ref, acc_ref, kbuf, vbuf)

    _attn_prompt_kernel(
        lam_ref, sg_ref, q_ref, k_ref, v_ref, gate_ref, o_ref, *p_scratch,
        tq=tq, tk=tk, unroll=unroll, lam_init=lam_init, before_diagonal=sample_pages)

    @pl.when(j == n_steps - 1)
    def _():
        _sample_finish(lam_ref, sg_ref, kn_ref, vn_ref, gate_s_ref, os_ref, qbd_ref, m_ref, l_ref,
                       acc_ref, t_len=t_len, lam_init=lam_init)


def _attention(page_table, lam_rows, subln_g, q_p, k_p, v_p, gate_p, n_seq_p, seq_len,
               q_s, k_new, v_new, gate_s, cache_k, cache_v, layer, n_seq_s, t_len, lam_init):
    assert t_len == SUBLANES and V_DIM == LANES
    tq, tk, unroll = 512, 512, 4
    assert tk % tq == 0 and seq_len % tk == 0 and unroll & (unroll - 1) == 0
    nq = seq_len // tq
    grid = (n_seq_p, N_HEADS, nq)
    n_grid = n_seq_p * N_HEADS * nq
    n_pages = page_table.shape[1]
    assert (n_seq_s * n_pages) % n_grid == 0
    g_pages = n_seq_s * n_pages // n_grid
    assert n_pages % g_pages == 0
    n_steps = n_pages // g_pages
    n_rows = N_HEADS * 2 * t_len
    pt_flat = page_table.reshape(-1)
    n_phys = cache_k.shape[1]
    ck = cache_k.reshape(-1, V_DIM)
    cv = cache_v.reshape(-1, V_DIM)

    def sample_pos(b, h, i):
        step = (b * N_HEADS + h) * nq + i
        return step // n_steps, step % n_steps

    q_spec = pl.BlockSpec((tq, V_DIM), lambda b, h, i, pt: (b * nq + i, h))
    kv_spec = pl.BlockSpec((seq_len, V_DIM), lambda b, h, i, pt: (b, h))
    tok_spec = pl.BlockSpec((t_len, D_ATTN), lambda b, h, i, pt: (sample_pos(b, h, i)[0], 0))
    new_spec = pl.BlockSpec((t_len * N_HEADS, V_DIM),
                            lambda b, h, i, pt: (sample_pos(b, h, i)[0], 0))
    full = lambda shape: pl.BlockSpec(shape, lambda b, h, i, pt: (0,) * len(shape))
    grid_spec = pltpu.PrefetchScalarGridSpec(
        num_scalar_prefetch=1,
        grid=grid,
        in_specs=[full((4, HEAD_DIM)), full((1, V_DIM)), q_spec, kv_spec, kv_spec, q_spec,
                  tok_spec, new_spec, new_spec, tok_spec,
                  pl.BlockSpec(memory_space=pl.ANY), pl.BlockSpec(memory_space=pl.ANY)],
        out_specs=(q_spec, tok_spec),
        scratch_shapes=[pltpu.VMEM((2 * tq, V_DIM), BF16),
                        pltpu.VMEM((2 * tq, LANES), F32),
                        pltpu.VMEM((2 * tq, LANES), F32),
                        pltpu.VMEM((2 * tq, V_DIM), F32),
                        pltpu.VMEM((n_rows, D_ATTN), BF16),
                        pltpu.VMEM((n_rows, LANES), F32),
                        pltpu.VMEM((n_rows, LANES), F32),
                        pltpu.VMEM((n_rows, D_ATTN), F32),
                        pltpu.VMEM((g_pages * PAGE_SIZE, D_ATTN), BF16),
                        pltpu.VMEM((g_pages * PAGE_SIZE, D_ATTN), BF16),
                        pltpu.VMEM((2, g_pages, PAGE_ROWS, V_DIM), F32),
                        pltpu.VMEM((2, g_pages, PAGE_ROWS, V_DIM), F32),
                        pltpu.SemaphoreType.DMA((2, 2))],
    )
    return pl.pallas_call(
        functools.partial(_attn_kernel, g_pages=g_pages, n_steps=n_steps, n_pages=n_pages,
                          n_grid=n_grid, row0=layer * n_phys * PAGE_ROWS, nq=nq, tq=tq, tk=tk,
                          unroll=unroll, t_len=t_len, lam_init=lam_init),
        out_shape=(jax.ShapeDtypeStruct((q_p.shape[0], D_ATTN), BF16),
                   jax.ShapeDtypeStruct((n_seq_s * t_len, D_ATTN), F32)),
        grid_spec=grid_spec,
        compiler_params=pltpu.CompilerParams(
            dimension_semantics=("arbitrary", "arbitrary", "arbitrary"),
            vmem_limit_bytes=VMEM_LIMIT),
        name="attention",
    )(pt_flat, lam_rows, subln_g, q_p, k_p, v_p, gate_p, q_s, k_new, v_new, gate_s, ck, cv)


def _outproj_kernel(x_ref, mc_ref, ma_ref, w_ref, g_ref, y_ref, *, final_norm):
    acc = jnp.dot(mc_ref[...].astype(BF16), w_ref[0:D_CONV, :], preferred_element_type=F32)
    acc = acc + jnp.dot(ma_ref[...].astype(BF16), w_ref[D_CONV:D_CONV + D_ATTN, :],
                        preferred_element_type=F32)
    y = x_ref[...] + acc
    y_ref[...] = _rmsnorm(y, g_ref[...]) if final_norm else y


def _outproj(x2d, mixc, mixa, w_bf, ln_f_g, final_norm):
    n_tok = x2d.shape[0]
    tm = 1024
    tok_spec = lambda width: pl.BlockSpec((tm, width), lambda t: (t, 0))
    full = lambda shape: pl.BlockSpec(shape, lambda t: (0,) * len(shape))
    return pl.pallas_call(
        functools.partial(_outproj_kernel, final_norm=final_norm),
        out_shape=jax.ShapeDtypeStruct((n_tok, D_MODEL), F32),
        grid=(n_tok // tm,),
        in_specs=[tok_spec(D_MODEL), tok_spec(D_CONV), tok_spec(D_ATTN),
                  full((D_CONV + D_ATTN, D_MODEL)), full((1, D_MODEL))],
        out_specs=tok_spec(D_MODEL),
        compiler_params=pltpu.CompilerParams(dimension_semantics=("arbitrary",),
                                             vmem_limit_bytes=VMEM_LIMIT),
        name="outproj",
    )(x2d, mixc, mixa, w_bf, ln_f_g)


def kernel(x_prompt, x_sample, cache_k, cache_v, state_conv, page_table, ln_in_g, w_in, conv_w,
           lambda_q1, lambda_k1, lambda_q2, lambda_k2, subln_g, w_out, ln_f_g):
    depth = w_in.shape[0]
    n_p, s_p, _ = x_prompt.shape
    n_s, t_s, _ = x_sample.shape
    xp = x_prompt.reshape(n_p * s_p, D_MODEL)
    xs = x_sample.reshape(n_s * t_s, D_MODEL)
    ln_f = ln_f_g.reshape(1, D_MODEL)
    kp_l, vp_l, cp_l, ks_l, vs_l, cs_l = [], [], [], [], [], []
    for layer in range(depth):
        lam_init = _lambda_init(layer)
        last = layer == depth - 1
        ln_g = ln_in_g[layer].reshape(1, D_MODEL)
        w_in_bf = w_in[layer].astype(BF16)
        w_out_bf = w_out[layer].astype(BF16)
        sg = subln_g[layer].reshape(1, V_DIM)
        lam_rows = jnp.stack([lambda_q1[layer], lambda_k1[layer],
                              lambda_q2[layer], lambda_k2[layer]])

        q_p, k_p, v_p, kb_p, vb_p, mixc_p, gate_p, conv_p = _inproj_prompt(
            xp, ln_g, w_in_bf, conv_w[layer], n_p, s_p)
        q_s, k_s, v_s, mixc_s, gate_s, conv_s = _inproj_sample(
            xs, ln_g, w_in_bf, conv_w[layer], state_conv[layer], n_s, t_s)
        mixa_p, mixa_s = _attention(
            page_table, lam_rows, sg, q_p, kb_p, vb_p, gate_p, n_p, s_p,
            q_s, k_s, v_s, gate_s, cache_k, cache_v, layer, n_s, t_s, lam_init)
        xp = _outproj(xp, mixc_p, mixa_p, w_out_bf, ln_f, last)
        xs = _outproj(xs, mixc_s, mixa_s, w_out_bf, ln_f, last)
        kp_l.append(k_p.reshape(n_p, s_p // PAGE_SIZE, PAGE_SIZE, N_HEADS, 2 * HEAD_DIM))
        vp_l.append(v_p.reshape(n_p, s_p // PAGE_SIZE, PAGE_SIZE, N_HEADS, V_DIM))
        cp_l.append(conv_p)
        ks_l.append(k_s.reshape(n_s, t_s, N_HEADS, 2 * HEAD_DIM))
        vs_l.append(v_s.reshape(n_s, t_s, N_HEADS, V_DIM))
        cs_l.append(conv_s)

    y_prompt = xp.reshape(n_p, s_p, D_MODEL)
    y_sample = xs.reshape(n_s, t_s, D_MODEL)
    return (y_prompt, y_sample, jnp.stack(kp_l), jnp.stack(vp_l), jnp.stack(cp_l),
            jnp.stack(ks_l), jnp.stack(vs_l), jnp.stack(cs_l))
```

```python
import functools
import math

import jax
import jax.numpy as jnp
from jax import lax
from jax.experimental import pallas as pl
from jax.experimental.pallas import tpu as pltpu

D_MODEL = 1024
D_CONV = 512
D_ATTN = 512
N_HEADS = 4
HEAD_DIM = 64
V_DIM = 2 * HEAD_DIM
CONV_W = 3
PAGE_SIZE = 128
PAGE_ROWS = PAGE_SIZE * N_HEADS
EPS = 1e-6
GROUP_W = 512
N_GROUPS = 8
SUBLANES = 8
LANES = 128
NEG_BIG = float("-inf")
LOG2_E = math.log2(math.e)
VMEM_LIMIT = 56 * 1024 * 1024

F32 = jnp.float32
BF16 = jnp.bfloat16


def _lambda_init(layer):
    return 0.8 - 0.6 * math.exp(-0.3 * layer)


def _silu(z):
    return z / (1.0 + jnp.exp(-z))


def _rmsnorm(x, g):
    ms = jnp.mean(x * x, axis=-1, keepdims=True)
    return x * lax.rsqrt(ms + EPS) * g


def _proj(hb, w_ref, group):
    w = w_ref[:, group * GROUP_W:(group + 1) * GROUP_W]
    return jnp.dot(hb, w, preferred_element_type=F32)


def _diff_lambda(lam_ref, lam_init):
    lv = lam_ref[...]
    a = jnp.sum(lv[0:1] * lv[1:2], axis=-1, keepdims=True)
    b = jnp.sum(lv[2:3] * lv[3:4], axis=-1, keepdims=True)
    return jnp.exp(a) - jnp.exp(b) + lam_init


def _lane_tile(x, n):
    return x if n == 1 else jnp.concatenate([x] * n, axis=1)


def _online_softmax(s, v_bf, m, l, acc, exp=jnp.exp):
    m_new = jnp.maximum(m, jnp.max(s, axis=-1, keepdims=True))
    alpha = exp(m - m_new)
    p = exp(s - _lane_tile(m_new, s.shape[1] // LANES))
    l = alpha * l + jnp.sum(p, axis=-1, keepdims=True)
    pv = jnp.dot(p.astype(BF16), v_bf, preferred_element_type=F32)
    acc = _lane_tile(alpha, acc.shape[1] // LANES) * acc + pv
    return m_new, l, acc


def _online_softmax_step(s, v_bf, m_ref, l_ref, acc_ref):
    m_ref[...], l_ref[...], acc_ref[...] = _online_softmax(
        s, v_bf, m_ref[...], l_ref[...], acc_ref[...])


def _store_head_rows(ref, x, n_tok):
    for h in range(N_HEADS):
        ref[pl.ds(h, n_tok, stride=N_HEADS), :] = x[:, h * V_DIM:(h + 1) * V_DIM]


def _load_head_rows(ref, n_tok):
    return jnp.concatenate(
        [ref[pl.ds(h, n_tok, stride=N_HEADS), :] for h in range(N_HEADS)], axis=1)


def _subln_gate(o, sg, gate, lam_init):
    ms = jnp.mean(o * o, axis=-1, keepdims=True)
    return gate * (o * lax.rsqrt(ms + EPS) * sg * (1.0 - lam_init))


def _inproj_prompt_kernel(x_ref, g_ref, w_ref, cw_ref,
                          q_ref, k_ref, v_ref, kb_ref, vb_ref, mixc_ref, gate_ref, conv_ref,
                          ubuf, *, tiles_per_seq, tm):
    t = pl.program_id(0)
    hb = _rmsnorm(x_ref[...], g_ref[...]).astype(BF16)

    @pl.when(t % tiles_per_seq == 0)
    def _():
        ubuf[0:SUBLANES, :] = jnp.zeros((SUBLANES, D_CONV), F32)

    u = _proj(hb, w_ref, 1) * _proj(hb, w_ref, 2)
    ubuf[SUBLANES:SUBLANES + tm, :] = u
    cw = cw_ref[...]
    y = (cw[0:1] * ubuf[SUBLANES - 2:SUBLANES - 2 + tm, :]
         + cw[1:2] * ubuf[SUBLANES - 1:SUBLANES - 1 + tm, :]
         + cw[2:3] * u)
    yc = _proj(hb, w_ref, 0) * y
    mixc_ref[...] = (_silu(_proj(hb, w_ref, 3)) * yc).astype(BF16)
    ubuf[0:SUBLANES, :] = u[tm - SUBLANES:tm, :]

    @pl.when(t % tiles_per_seq == tiles_per_seq - 1)
    def _():
        conv_ref[0] = u[tm - (CONV_W - 1):tm, :]

    q_ref[...] = (_proj(hb, w_ref, 4) * (HEAD_DIM ** -0.5 * LOG2_E)).astype(BF16)
    k = _proj(hb, w_ref, 5)
    _store_head_rows(k_ref, k, tm)
    kb_ref[...] = k.astype(BF16)
    v = _proj(hb, w_ref, 6)
    _store_head_rows(v_ref, v, tm)
    vb_ref[...] = v.astype(BF16)
    gate_ref[...] = _silu(_proj(hb, w_ref, 7)).astype(BF16)


def _inproj_prompt(x2d, ln_g, w_bf, conv_w, n_seq, seq_len):
    n_tok = x2d.shape[0]
    tm = 1024
    tiles_per_seq = seq_len // tm
    n_tiles = n_tok // tm
    tok_spec = lambda width: pl.BlockSpec((tm, width), lambda t: (t, 0))
    head_rows = pl.BlockSpec((tm * N_HEADS, V_DIM), lambda t: (t, 0))
    full = lambda shape: pl.BlockSpec(shape, lambda t: (0,) * len(shape))
    out_shape = (
        jax.ShapeDtypeStruct((n_tok, D_ATTN), BF16),
        jax.ShapeDtypeStruct((n_tok * N_HEADS, V_DIM), F32),
        jax.ShapeDtypeStruct((n_tok * N_HEADS, V_DIM), F32),
        jax.ShapeDtypeStruct((n_tok, D_ATTN), BF16),
        jax.ShapeDtypeStruct((n_tok, D_ATTN), BF16),
        jax.ShapeDtypeStruct((n_tok, D_CONV), BF16),
        jax.ShapeDtypeStruct((n_tok, D_ATTN), BF16),
        jax.ShapeDtypeStruct((n_seq, CONV_W - 1, D_CONV), F32),
    )
    return pl.pallas_call(
        functools.partial(_inproj_prompt_kernel, tiles_per_seq=tiles_per_seq, tm=tm),
        out_shape=out_shape,
        grid=(n_tiles,),
        in_specs=[tok_spec(D_MODEL), full((1, D_MODEL)), full((D_MODEL, N_GROUPS * GROUP_W)),
                  full((CONV_W, D_CONV))],
        out_specs=(tok_spec(D_ATTN), head_rows, head_rows) + (tok_spec(D_ATTN),) * 4 + (
            pl.BlockSpec((1, CONV_W - 1, D_CONV), lambda t: (t // tiles_per_seq, 0, 0)),),
        scratch_shapes=[pltpu.VMEM((tm + SUBLANES, D_CONV), F32)],
        compiler_params=pltpu.CompilerParams(dimension_semantics=("arbitrary",),
                                             vmem_limit_bytes=VMEM_LIMIT),
        name="inproj_prompt",
    )(x2d, ln_g, w_bf, conv_w)


def _inproj_sample_kernel(x_ref, g_ref, w_ref, cw_ref, buf_ref,
                          q_ref, k_ref, v_ref, mixc_ref, gate_ref, conv_ref, *, n_seq, t_len):
    hb = _rmsnorm(x_ref[...], g_ref[...]).astype(BF16)
    n_tok = n_seq * t_len
    u3 = (_proj(hb, w_ref, 1) * _proj(hb, w_ref, 2)).reshape(n_seq, t_len, D_CONV)
    tpos = lax.broadcasted_iota(jnp.int32, (n_seq, t_len, D_CONV), 1)
    b0 = buf_ref[:, 0:1, :]
    b1 = buf_ref[:, 1:2, :]
    prev1 = jnp.where(tpos >= 1, pltpu.roll(u3, 1, 1), b1)
    prev2 = jnp.where(tpos >= 2, pltpu.roll(u3, 2, 1), jnp.where(tpos == 1, b1, b0))
    cw = cw_ref[...]
    y = cw[0:1][None] * prev2 + cw[1:2][None] * prev1 + cw[2:3][None] * u3
    yc = _proj(hb, w_ref, 0) * y.reshape(n_tok, D_CONV)
    mixc_ref[...] = _silu(_proj(hb, w_ref, 3)) * yc
    conv_ref[...] = u3[:, t_len - (CONV_W - 1):t_len, :]
    q_ref[...] = _proj(hb, w_ref, 4) * (HEAD_DIM ** -0.5)
    _store_head_rows(k_ref, _proj(hb, w_ref, 5), n_tok)
    _store_head_rows(v_ref, _proj(hb, w_ref, 6), n_tok)
    gate_ref[...] = _silu(_proj(hb, w_ref, 7))


def _inproj_sample(x2d, ln_g, w_bf, conv_w, buf, n_seq, t_len):
    assert t_len == SUBLANES
    n_tok = x2d.shape[0]
    tok = jax.ShapeDtypeStruct((n_tok, D_ATTN), F32)
    head_rows = jax.ShapeDtypeStruct((n_tok * N_HEADS, V_DIM), F32)
    return pl.pallas_call(
        functools.partial(_inproj_sample_kernel, n_seq=n_seq, t_len=t_len),
        out_shape=(tok, head_rows, head_rows, tok, tok,
                   jax.ShapeDtypeStruct((n_seq, CONV_W - 1, D_CONV), F32)),
        compiler_params=pltpu.CompilerParams(vmem_limit_bytes=VMEM_LIMIT),
        name="inproj_sample",
    )(x2d, ln_g, w_bf, conv_w, buf)


def _attn_prompt_kernel(lam_ref, sg_ref, q_ref, k_ref, v_ref, gate_ref, o_ref,
                        qs_ref, m_ref, l_ref, acc_ref, *, tq, tk, unroll, lam_init,
                        before_diagonal=None):
    qi = pl.program_id(2)
    q_start = qi * tq
    q = q_ref[...]
    lane = lax.broadcasted_iota(jnp.int32, (tq, V_DIM), 1)
    zero = jnp.zeros_like(q)
    qs_ref[0:tq, :] = jnp.where(lane < HEAD_DIM, q, zero)
    qs_ref[tq:2 * tq, :] = jnp.where(lane >= HEAD_DIM, q, zero)
    m_ref[...] = jnp.full(m_ref.shape, NEG_BIG, F32)
    l_ref[...] = jnp.zeros(l_ref.shape, F32)
    acc_ref[...] = jnp.zeros(acc_ref.shape, F32)

    def scores(ki):
        k_off = pl.multiple_of(ki * tk, tk)
        kblk = k_ref[pl.ds(k_off, tk), :]
        vblk = v_ref[pl.ds(k_off, tk), :]
        s = lax.dot_general(qs_ref[...], kblk, (((1,), (1,)), ((), ())),
                            preferred_element_type=F32)
        return s, vblk

    def full_tiles(first, count):
        state = (m_ref[...], l_ref[...], acc_ref[...])
        for u in range(count):
            s, vblk = scores(first + u)
            state = _online_softmax(s, vblk, *state, exp=jnp.exp2)
        m_ref[...], l_ref[...], acc_ref[...] = state

    def group(gi, carry):
        full_tiles(gi * unroll, unroll)
        return carry

    n_full = q_start // tk
    n_groups = n_full // unroll
    lax.fori_loop(0, n_groups, group, 0)
    done = n_groups * unroll
    part = unroll // 2
    while part >= 1:
        take = (n_full - done) & part

        @pl.when(take != 0)
        def _(done=done, part=part):
            full_tiles(done, part)

        done = done + take
        part //= 2

    if before_diagonal is not None:
        before_diagonal()

    s, vblk = scores(n_full)
    row = lax.broadcasted_iota(jnp.int32, (2 * tq, tk), 0)
    col = lax.broadcasted_iota(jnp.int32, (2 * tq, tk), 1)
    q_pos = q_start + jnp.where(row >= tq, row - tq, row)
    s = jnp.where(n_full * tk + col <= q_pos, s, NEG_BIG)
    _, l, acc = _online_softmax(s, vblk, m_ref[...], l_ref[...], acc_ref[...], exp=jnp.exp2)

    o = acc[0:tq] / l[0:tq] - _diff_lambda(lam_ref, lam_init) * (acc[tq:2 * tq] / l[tq:2 * tq])
    o_ref[...] = _subln_gate(o, sg_ref[...], gate_ref[...].astype(F32), lam_init).astype(BF16)


def _sample_init(q_ref, qbd_ref, m_ref, l_ref, acc_ref, *, t_len):
    n_rows = N_HEADS * 2 * t_len
    qt = jnp.concatenate([q_ref[...]] * (N_HEADS * 2), axis=0)
    row = lax.broadcasted_iota(jnp.int32, (n_rows, D_ATTN), 0)
    col = lax.broadcasted_iota(jnp.int32, (n_rows, D_ATTN), 1)
    keep = (col // HEAD_DIM) == (row // t_len)
    qbd_ref[...] = jnp.where(keep, qt, 0.0).astype(BF16)
    m_ref[...] = jnp.full(m_ref.shape, NEG_BIG, F32)
    l_ref[...] = jnp.zeros(l_ref.shape, F32)
    acc_ref[...] = jnp.zeros(acc_ref.shape, F32)


def _sample_pages(k_refs, v_refs, qbd_ref, m_ref, l_ref, acc_ref, kbuf, vbuf):
    for g in range(len(k_refs)):
        rows = slice(g * PAGE_SIZE, (g + 1) * PAGE_SIZE)
        kbuf[rows, :] = _load_head_rows(k_refs[g], PAGE_SIZE).astype(BF16)
        vbuf[rows, :] = _load_head_rows(v_refs[g], PAGE_SIZE).astype(BF16)
    s = lax.dot_general(qbd_ref[...], kbuf[...], (((1,), (1,)), ((), ())),
                        preferred_element_type=F32)
    _online_softmax_step(s, vbuf[...], m_ref, l_ref, acc_ref)


def _sample_finish(lam_ref, sg_ref, kn_ref, vn_ref, gate_ref, o_ref, qbd_ref, m_ref, l_ref,
                   acc_ref, *, t_len, lam_init):
    n_rows = N_HEADS * 2 * t_len
    pad = jnp.zeros((LANES - t_len, D_ATTN), F32)
    kn = jnp.concatenate([_load_head_rows(kn_ref, t_len), pad], axis=0).astype(BF16)
    vn = jnp.concatenate([_load_head_rows(vn_ref, t_len), pad], axis=0).astype(BF16)
    sn = lax.dot_general(qbd_ref[...], kn, (((1,), (1,)), ((), ())),
                         preferred_element_type=F32)
    row = lax.broadcasted_iota(jnp.int32, (n_rows, LANES), 0)
    col = lax.broadcasted_iota(jnp.int32, (n_rows, LANES), 1)
    sn = jnp.where(col <= row % t_len, sn, NEG_BIG)
    _, l, acc = _online_softmax(sn, vn, m_ref[...], l_ref[...], acc_ref[...])

    lam = _diff_lambda(lam_ref, lam_init)
    sg = sg_ref[...]
    gate = gate_ref[...]
    for h in range(N_HEADS):
        r1 = (2 * h) * t_len
        r2 = (2 * h + 1) * t_len
        lanes = slice(h * V_DIM, (h + 1) * V_DIM)
        o = (acc[r1:r1 + t_len, lanes] / l[r1:r1 + t_len]
             - lam * (acc[r2:r2 + t_len, lanes] / l[r2:r2 + t_len]))
        o_ref[:, lanes] = _subln_gate(o, sg, gate[:, lanes], lam_init)


def _page_copies(pt_ref, cache_hbm, slot_ref, sem, step, *, g_pages, n_steps, n_pages, row0):
    seq = step // n_steps
    first = seq * n_pages + (step % n_steps) * g_pages
    copies = []
    for g in range(g_pages):
        rows = pl.multiple_of(row0 + pt_ref[first + g] * PAGE_ROWS, PAGE_ROWS)
        copies.append(pltpu.make_async_copy(
            cache_hbm.at[pl.ds(rows, PAGE_ROWS), :], slot_ref.at[g], sem))
    return copies


def _attn_kernel(pt_ref, lam_ref, sg_ref, q_ref, k_ref, v_ref, gate_ref,
                 qs_ref, kn_ref, vn_ref, gate_s_ref, ck_hbm, cv_hbm, o_ref, os_ref, *scratch,
                 g_pages, n_steps, n_pages, n_grid, row0, nq, tq, tk, unroll, t_len, lam_init):
    p_scratch = scratch[:4]
    qbd_ref, m_ref, l_ref, acc_ref, kbuf, vbuf, kpg, vpg, sem = scratch[4:]
    step = (pl.program_id(0) * N_HEADS + pl.program_id(1)) * nq + pl.program_id(2)
    j = step % n_steps
    slot = step % 2

    def copies(of_step, of_slot):
        kw = dict(g_pages=g_pages, n_steps=n_steps, n_pages=n_pages, row0=row0)
        return (_page_copies(pt_ref, ck_hbm, kpg.at[of_slot], sem.at[of_slot, 0], of_step, **kw)
                + _page_copies(pt_ref, cv_hbm, vpg.at[of_slot], sem.at[of_slot, 1], of_step, **kw))

    @pl.when(step == 0)
    def _():
        for c in copies(step, slot):
            c.start()

    @pl.when(step + 1 < n_grid)
    def _():
        for c in copies(step + 1, 1 - slot):
            c.start()

    @pl.when(j == 0)
    def _():
        _sample_init(qs_ref, qbd_ref, m_ref, l_ref, acc_ref, t_len=t_len)

    def sample_pages():
        for c in copies(step, slot):
            c.wait()
        k_slot, v_slot = kpg.at[slot], vpg.at[slot]
        _sample_pages([k_slot.at[g] for g in range(g_pages)],
                      [v_slot.at[g] for g in range(g_pages)],
                      qbd_ref, m_ref, l_ref, acc_ref, kbuf, vbuf)

    _attn_prompt_kernel(
        lam_ref, sg_ref, q_ref, k_ref, v_ref, gate_ref, o_ref, *p_scratch,
        tq=tq, tk=tk, unroll=unroll, lam_init=lam_init, before_diagonal=sample_pages)

    @pl.when(j == n_steps - 1)
    def _():
        _sample_finish(lam_ref, sg_ref, kn_ref, vn_ref, gate_s_ref, os_ref, qbd_ref, m_ref, l_ref,
                       acc_ref, t_len=t_len, lam_init=lam_init)


def _attention(page_table, lam_rows, subln_g, q_p, k_p, v_p, gate_p, n_seq_p, seq_len,
               q_s, k_new, v_new, gate_s, cache_k, cache_v, layer, n_seq_s, t_len, lam_init):
    assert t_len == SUBLANES and V_DIM == LANES
    tq, tk, unroll = 512, 512, 4
    assert tk % tq == 0 and seq_len % tk == 0 and unroll & (unroll - 1) == 0
    nq = seq_len // tq
    grid = (n_seq_p, N_HEADS, nq)
    n_grid = n_seq_p * N_HEADS * nq
    n_pages = page_table.shape[1]
    assert (n_seq_s * n_pages) % n_grid == 0
    g_pages = n_seq_s * n_pages // n_grid
    assert n_pages % g_pages == 0
    n_steps = n_pages // g_pages
    n_rows = N_HEADS * 2 * t_len
    pt_flat = page_table.reshape(-1)
    n_phys = cache_k.shape[1]
    ck = cache_k.reshape(-1, V_DIM)
    cv = cache_v.reshape(-1, V_DIM)

    def sample_pos(b, h, i):
        step = (b * N_HEADS + h) * nq + i
        return step // n_steps, step % n_steps

    q_spec = pl.BlockSpec((tq, V_DIM), lambda b, h, i, pt: (b * nq + i, h))
    kv_spec = pl.BlockSpec((seq_len, V_DIM), lambda b, h, i, pt: (b, h))
    tok_spec = pl.BlockSpec((t_len, D_ATTN), lambda b, h, i, pt: (sample_pos(b, h, i)[0], 0))
    new_spec = pl.BlockSpec((t_len * N_HEADS, V_DIM),
                            lambda b, h, i, pt: (sample_pos(b, h, i)[0], 0))
    full = lambda shape: pl.BlockSpec(shape, lambda b, h, i, pt: (0,) * len(shape))
    grid_spec = pltpu.PrefetchScalarGridSpec(
        num_scalar_prefetch=1,
        grid=grid,
        in_specs=[full((4, HEAD_DIM)), full((1, V_DIM)), q_spec, kv_spec, kv_spec, q_spec,
                  tok_spec, new_spec, new_spec, tok_spec,
                  pl.BlockSpec(memory_space=pl.ANY), pl.BlockSpec(memory_space=pl.ANY)],
        out_specs=(q_spec, tok_spec),
        scratch_shapes=[pltpu.VMEM((2 * tq, V_DIM), BF16),
                        pltpu.VMEM((2 * tq, LANES), F32),
                        pltpu.VMEM((2 * tq, LANES), F32),
                        pltpu.VMEM((2 * tq, V_DIM), F32),
                        pltpu.VMEM((n_rows, D_ATTN), BF16),
                        pltpu.VMEM((n_rows, LANES), F32),
                        pltpu.VMEM((n_rows, LANES), F32),
                        pltpu.VMEM((n_rows, D_ATTN), F32),
                        pltpu.VMEM((g_pages * PAGE_SIZE, D_ATTN), BF16),
                        pltpu.VMEM((g_pages * PAGE_SIZE, D_ATTN), BF16),
                        pltpu.VMEM((2, g_pages, PAGE_ROWS, V_DIM), F32),
                        pltpu.VMEM((2, g_pages, PAGE_ROWS, V_DIM), F32),
                        pltpu.SemaphoreType.DMA((2, 2))],
    )
    return pl.pallas_call(
        functools.partial(_attn_kernel, g_pages=g_pages, n_steps=n_steps, n_pages=n_pages,
                          n_grid=n_grid, row0=layer * n_phys * PAGE_ROWS, nq=nq, tq=tq, tk=tk,
                          unroll=unroll, t_len=t_len, lam_init=lam_init),
        out_shape=(jax.ShapeDtypeStruct((q_p.shape[0], D_ATTN), BF16),
                   jax.ShapeDtypeStruct((n_seq_s * t_len, D_ATTN), F32)),
        grid_spec=grid_spec,
        compiler_params=pltpu.CompilerParams(
            dimension_semantics=("arbitrary", "arbitrary", "arbitrary"),
            vmem_limit_bytes=VMEM_LIMIT),
        name="attention",
    )(pt_flat, lam_rows, subln_g, q_p, k_p, v_p, gate_p, q_s, k_new, v_new, gate_s, ck, cv)


def _outproj_kernel(x_ref, mc_ref, ma_ref, w_ref, g_ref, y_ref, *, final_norm):
    mix = jnp.concatenate([mc_ref[...].astype(BF16), ma_ref[...].astype(BF16)], axis=1)
    y = x_ref[...] + jnp.dot(mix, w_ref[...], preferred_element_type=F32)
    y_ref[...] = _rmsnorm(y, g_ref[...]) if final_norm else y


def _outproj(x2d, mixc, mixa, w_bf, ln_f_g, final_norm):
    n_tok = x2d.shape[0]
    tm = 1024
    tok_spec = lambda width: pl.BlockSpec((tm, width), lambda t: (t, 0))
    full = lambda shape: pl.BlockSpec(shape, lambda t: (0,) * len(shape))
    return pl.pallas_call(
        functools.partial(_outproj_kernel, final_norm=final_norm),
        out_shape=jax.ShapeDtypeStruct((n_tok, D_MODEL), F32),
        grid=(n_tok // tm,),
        in_specs=[tok_spec(D_MODEL), tok_spec(D_CONV), tok_spec(D_ATTN),
                  full((D_CONV + D_ATTN, D_MODEL)), full((1, D_MODEL))],
        out_specs=tok_spec(D_MODEL),
        compiler_params=pltpu.CompilerParams(dimension_semantics=("arbitrary",),
                                             vmem_limit_bytes=VMEM_LIMIT),
        name="outproj",
    )(x2d, mixc, mixa, w_bf, ln_f_g)


def kernel(x_prompt, x_sample, cache_k, cache_v, state_conv, page_table, ln_in_g, w_in, conv_w,
           lambda_q1, lambda_k1, lambda_q2, lambda_k2, subln_g, w_out, ln_f_g):
    depth = w_in.shape[0]
    n_p, s_p, _ = x_prompt.shape
    n_s, t_s, _ = x_sample.shape
    xp = x_prompt.reshape(n_p * s_p, D_MODEL)
    xs = x_sample.reshape(n_s * t_s, D_MODEL)
    ln_f = ln_f_g.reshape(1, D_MODEL)
    kp_l, vp_l, cp_l, ks_l, vs_l, cs_l = [], [], [], [], [], []
    for layer in range(depth):
        lam_init = _lambda_init(layer)
        last = layer == depth - 1
        ln_g = ln_in_g[layer].reshape(1, D_MODEL)
        w_in_bf = w_in[layer].astype(BF16)
        w_out_bf = w_out[layer].astype(BF16)
        sg = subln_g[layer].reshape(1, V_DIM)
        lam_rows = jnp.stack([lambda_q1[layer], lambda_k1[layer],
                              lambda_q2[layer], lambda_k2[layer]])

        q_p, k_p, v_p, kb_p, vb_p, mixc_p, gate_p, conv_p = _inproj_prompt(
            xp, ln_g, w_in_bf, conv_w[layer], n_p, s_p)
        q_s, k_s, v_s, mixc_s, gate_s, conv_s = _inproj_sample(
            xs, ln_g, w_in_bf, conv_w[layer], state_conv[layer], n_s, t_s)
        mixa_p, mixa_s = _attention(
            page_table, lam_rows, sg, q_p, kb_p, vb_p, gate_p, n_p, s_p,
            q_s, k_s, v_s, gate_s, cache_k, cache_v, layer, n_s, t_s, lam_init)
        xp = _outproj(xp, mixc_p, mixa_p, w_out_bf, ln_f, last)
        xs = _outproj(xs, mixc_s, mixa_s, w_out_bf, ln_f, last)
        kp_l.append(k_p.reshape(n_p, s_p // PAGE_SIZE, PAGE_SIZE, N_HEADS, 2 * HEAD_DIM))
        vp_l.append(v_p.reshape(n_p, s_p // PAGE_SIZE, PAGE_SIZE, N_HEADS, V_DIM))
        cp_l.append(conv_p)
        ks_l.append(k_s.reshape(n_s, t_s, N_HEADS, 2 * HEAD_DIM))
        vs_l.append(v_s.reshape(n_s, t_s, N_HEADS, V_DIM))
        cs_l.append(conv_s)

    y_prompt = xp.reshape(n_p, s_p, D_MODEL)
    y_sample = xs.reshape(n_s, t_s, D_MODEL)
    return (y_prompt, y_sample, jnp.stack(kp_l), jnp.stack(vp_l), jnp.stack(cp_l),
            jnp.stack(ks_l), jnp.stack(vs_l), jnp.stack(cs_l))
```
